```python
import functools
import jax, jax.numpy as jnp
from jax import lax
import numpy as np

D_MODEL = 2048
BATCH = 2
SEQ = 4096
DEPTH = 4
DEC_BATCH = 128
DEC_SEQ = 8
PAST_LEN = 8192
PAGE_SIZE = 128

HG_HEADS = 4
HG_DK = 128
HG_DV = 128
HG_WIDTH = HG_HEADS * HG_DV
HG_CHUNK = 16
NSA_HEADS = 8
NSA_HD = 64
NSA_WIDTH = NSA_HEADS * NSA_HD
CMP_BLOCK = 32
SEL_BLOCK = 64
N_SEL = 16
WINDOW = 512
FORCED_IMPORTANCE = 1e4
MLA_HEADS = 8
MLA_NOPE = 64
MLA_ROPE = 32
MLA_VD = 64
KV_LORA = 256
MLA_WIDTH = MLA_HEADS * MLA_VD
ROPE_THETA = 500000.0
NSA_ROT = NSA_HD // 4
D_FF = 5632
N_BRANCH = 3
MIX_WIDTH = HG_WIDTH + NSA_WIDTH + MLA_WIDTH
Q_BLOCK = 128
EPS = 1e-6
NEG_INF = -1e30

IN_SPLITS = (HG_HEADS * HG_DK, HG_HEADS * HG_DK, HG_WIDTH, HG_WIDTH,
             NSA_WIDTH, 2 * NSA_HD, 2 * NSA_HD, 2 * NSA_HD, 3 * NSA_HEADS,
             MLA_HEADS * (MLA_NOPE + MLA_ROPE), KV_LORA, MLA_ROPE,
             N_BRANCH * D_MODEL)
D_IN = sum(IN_SPLITS)

kernel_name = 'hybrid_hgrn2_nsa_mla_decode_step'


def rmsnorm(x, g):
    xf = x.astype(jnp.float32)
    y = xf * lax.rsqrt(jnp.mean(xf * xf, axis=-1, keepdims=True) + EPS)
    return (y * g.astype(jnp.float32)).astype(x.dtype)


def rope(x, pos, rot_dim):
    half = rot_dim // 2
    inv_freq = ROPE_THETA ** (-jnp.arange(half, dtype=jnp.float32) / half)
    ang = pos.astype(jnp.float32)[:, None] * inv_freq[None, :]
    shape = (1, pos.shape[0]) + (1,) * (x.ndim - 3) + (half,)
    cos, sin = jnp.cos(ang).reshape(shape), jnp.sin(ang).reshape(shape)
    xf = x.astype(jnp.float32)
    x1, x2 = xf[..., :half], xf[..., half:rot_dim]
    out = jnp.concatenate([x1 * cos - x2 * sin, x2 * cos + x1 * sin, xf[..., rot_dim:]], axis=-1)
    return out.astype(x.dtype)


def masked_softmax(s, mask):
    s = jnp.where(mask, s.astype(jnp.float32), NEG_INF)
    return jax.nn.softmax(s, axis=-1) * mask


def swiglu(h, w_gu, w_dn):
    a, b = jnp.split(h @ w_gu, 2, axis=-1)
    return (jax.nn.silu(a) * b) @ w_dn


def sandwich_ffn(x, g_pre, g_post, w_gu, w_dn):
    return x + 0.5 * rmsnorm(swiglu(rmsnorm(x, g_pre), w_gu, w_dn), g_post)


def hgrn_lower_bounds(lb_param):
    p = jax.nn.softmax(lb_param.astype(jnp.float32), axis=0)
    return jnp.cumsum(p, axis=0) - p[0]


def hgrn2_chunked(q, log_f, k, v, s0):
    B, T, H, _ = q.shape
    C = HG_CHUNK
    Tp = -(-T // C) * C
    pad = ((0, 0), (0, Tp - T), (0, 0), (0, 0))
    q, log_f, k, v = [jnp.pad(a, pad) for a in (q, log_f, k, v)]
    n = Tp // C

    def to_chunks(a):
        return a.reshape(B, n, C, H, a.shape[-1]).transpose(1, 0, 3, 2, 4)

    qc, bc, kc, vc = to_chunks(q), jnp.cumsum(to_chunks(log_f), axis=3), to_chunks(k), to_chunks(v)
    causal = jnp.tril(jnp.ones((C, C), bool))[:, :, None]

    def step(S, inp):
        qi, bi, ki, vi = inp
        o_inter = jnp.einsum('bhtk,bhkv->bhtv', qi * jnp.exp(bi), S)
        diff = bi[:, :, :, None, :] - bi[:, :, None, :, :]
        decay = jnp.exp(jnp.where(causal, diff, NEG_INF))
        A = jnp.einsum('bhtk,bhtsk,bhsk->bhts', qi, decay, ki)
        o = o_inter + jnp.einsum('bhts,bhsv->bhtv', A, vi)
        b_last = bi[:, :, -1, :]
        S_new = jnp.exp(b_last)[..., None] * S + jnp.einsum(
            'bhsk,bhsv->bhkv', ki * jnp.exp(b_last[:, :, None, :] - bi), vi)
        return S_new, o

    S, o = lax.scan(step, s0, (qc, bc, kc, vc))
    o = o.transpose(1, 0, 3, 2, 4).reshape(B, Tp, H, -1)[:, :T]
    return o, S


def hgrn_readout(o, g, gain):
    B, T = g.shape[:2]
    o = rmsnorm(o, gain).reshape(B, T, HG_WIDTH)
    return (o * jax.nn.silu(g.astype(jnp.float32))).astype(g.dtype)


def nsa_block_keys(kv_cmp, kv_sel):
    B, L = kv_cmp.shape[:2]
    Lp = -(-L // SEL_BLOCK) * SEL_BLOCK
    pad = ((0, 0), (0, Lp - L), (0, 0), (0, 0))
    kv_cmp, kv_sel = jnp.pad(kv_cmp, pad), jnp.pad(kv_sel, pad)
    means = kv_cmp.astype(jnp.float32).reshape(B, Lp // CMP_BLOCK, CMP_BLOCK, 2, NSA_HD).mean(2)
    means = means.astype(kv_cmp.dtype)
    blocks = kv_sel.reshape(B, Lp // SEL_BLOCK, SEL_BLOCK, 2, NSA_HD)
    return means[:, :, 0], means[:, :, 1], blocks[:, :, :, 0], blocks[:, :, :, 1]


def nsa_attend(q, gates, qpos, kc, vc, ks_blk, vs_blk, kw, vw, wpos):
    scale = NSA_HD ** -0.5
    B, Q, H, _ = q.shape
    NC, NS = kc.shape[1], ks_blk.shape[1]
    s_c = jnp.einsum('bqhd,bnd->bqhn', q, kc) * scale
    cmp_end = (jnp.arange(NC) + 1) * CMP_BLOCK - 1
    p_c = masked_softmax(s_c, (cmp_end[None, :] <= qpos[:, None])[None, :, None, :])
    o_c = jnp.einsum('bqhn,bnd->bqhd', p_c.astype(vc.dtype), vc)
    imp = p_c.sum(2).reshape(B, Q, NS, SEL_BLOCK // CMP_BLOCK).sum(-1)
    blk = jnp.arange(NS)[None, :]
    cur = (qpos // SEL_BLOCK)[:, None]
    forced = (blk == 0) | (blk == cur) | (blk == cur - 1)
    imp = jnp.where(forced, imp + FORCED_IMPORTANCE, imp)
    imp = jnp.where(blk * SEL_BLOCK <= qpos[:, None], imp, NEG_INF)
    _, idx = lax.top_k(imp, min(N_SEL, NS))
    bidx = jnp.arange(B)[:, None, None]
    ks, vs = ks_blk[bidx, idx], vs_blk[bidx, idx]
    tok = idx[..., None] * SEL_BLOCK + jnp.arange(SEL_BLOCK)
    m_s = tok <= qpos[None, :, None, None]
    s_s = jnp.einsum('bqhd,bqnkd->bqhnk', q, ks) * scale
    p_s = masked_softmax(s_s.reshape(B, Q, H, -1), m_s.reshape(B, Q, 1, -1)).reshape(s_s.shape)
    o_s = jnp.einsum('bqhnk,bqnkd->bqhd', p_s.astype(vs.dtype), vs)
    s_w = jnp.einsum('bqhd,bwd->bqhw', q, kw) * scale
    m_w = (wpos[None, :] <= qpos[:, None]) & (wpos[None, :] > qpos[:, None] - WINDOW) & (wpos[None, :] >= 0)
    p_w = masked_softmax(s_w, m_w[None, :, None, :])
    o_w = jnp.einsum('bqhw,bwd->bqhd', p_w.astype(vw.dtype), vw)
    return gates[..., 0:1] * o_c + gates[..., 1:2] * o_s + gates[..., 2:3] * o_w


def mla_attend(q_lat, q_rope, c, kr, qpos, kpos):
    scale = (MLA_NOPE + MLA_ROPE) ** -0.5
    s = (jnp.einsum('bqhc,blc->bqhl', q_lat, c) + jnp.einsum('bqhr,blr->bqhl', q_rope, kr)) * scale
    p = masked_softmax(s, (kpos[None, :] <= qpos[:, None])[None, :, None, :])
    return jnp.einsum('bqhl,blc->bqhc', p.astype(c.dtype), c)


def mla_up(o_lat, w_uv):
    B, T = o_lat.shape[:2]
    return jnp.einsum('bthc,chv->bthv', o_lat, w_uv).reshape(B, T, MLA_WIDTH)


def qslice(a, s):
    return lax.dynamic_slice_in_dim(a, s, Q_BLOCK, axis=1)


def sweep_query_blocks(fn, T):
    o = lax.map(fn, jnp.arange(T // Q_BLOCK) * Q_BLOCK)
    return jnp.moveaxis(o, 0, 1).reshape((o.shape[1], T) + o.shape[3:])


def gather_pages(pool, page_table):
    return pool[page_table].reshape((page_table.shape[0], -1) + pool.shape[2:])


def project_mixer_inputs(z, pos, w_in, lb, mla_gain, w_uk):
    B, T, _ = z.shape
    f32 = jnp.float32
    cuts = np.cumsum(IN_SPLITS)[:-1].tolist()
    (hq, hf, hi, hg, nq, ncmp, nsel, nwin, ngate, mq, mc, mkr, mgate) = jnp.split(z @ w_in, cuts, axis=-1)
    lbh = lb.reshape(HG_HEADS, HG_DK)
    q = jax.nn.silu(hq.astype(f32)).reshape(B, T, HG_HEADS, HG_DK)
    hf = hf.astype(f32).reshape(B, T, HG_HEADS, HG_DK)
    log_f = jnp.log(lbh + (1.0 - lbh) * jax.nn.sigmoid(hf))
    k = (1.0 - lbh) * jax.nn.sigmoid(-hf)
    v = hi.astype(f32).reshape(B, T, HG_HEADS, HG_DV)
    hgrn = (q, log_f, k, v, hg)

    def kv_rows(a):
        kk, vv = jnp.split(a, 2, axis=-1)
        return jnp.stack([rope(kk, pos, NSA_ROT), vv], axis=2)

    nsa = (rope(nq.reshape(B, T, NSA_HEADS, NSA_HD), pos, NSA_ROT),
           jax.nn.sigmoid(ngate.reshape(B, T, NSA_HEADS, 3)),
           kv_rows(ncmp), kv_rows(nsel), kv_rows(nwin))
    mq = mq.reshape(B, T, MLA_HEADS, MLA_NOPE + MLA_ROPE)
    q_lat = jnp.einsum('bthn,chn->bthc', mq[..., :MLA_NOPE], w_uk)
    mla = (q_lat, rope(mq[..., MLA_NOPE:], pos, MLA_ROPE), rmsnorm(mc, mla_gain), rope(mkr, pos, MLA_ROPE))
    return hgrn, nsa, mla, mgate


def merge_branches(o_hg, o_nsa, o_mla, gate_pre, w_br, w_out):
    B, T = gate_pre.shape[:2]
    gates = jax.nn.sigmoid(gate_pre).reshape(B, T, N_BRANCH, D_MODEL)
    y = 0.0
    r = 0
    for m, o in enumerate((o_hg, o_nsa, o_mla)):
        w = o.shape[-1]
        y = y + gates[:, :, m] * (o @ w_br[r:r + w])
        r += w
    return y @ w_out


def mixer_prompt(hgrn, nsa, mla, hg_gain, w_uv):
    q, log_f, k, v, g = hgrn
    B, T = g.shape[:2]
    o, S = hgrn2_chunked(q, log_f, k, v, jnp.zeros((B, HG_HEADS, HG_DK, HG_DV), jnp.float32))
    o_hg = hgrn_readout(o, g, hg_gain)

    nq, ng, kv_cmp, kv_sel, kv_win = nsa
    kc, vc, ks, vs = nsa_block_keys(kv_cmp, kv_sel)
    w_pad = jnp.pad(kv_win, ((0, 0), (WINDOW, 0), (0, 0), (0, 0)))

    def nsa_blk(s):
        qpos = s + jnp.arange(Q_BLOCK)
        w = lax.dynamic_slice_in_dim(w_pad, s, WINDOW + Q_BLOCK, axis=1)
        wpos = s - WINDOW + jnp.arange(WINDOW + Q_BLOCK)
        return nsa_attend(qslice(nq, s), qslice(ng, s), qpos, kc, vc, ks, vs, w[:, :, 0], w[:, :, 1], wpos)

    o_nsa = sweep_query_blocks(nsa_blk, T).reshape(B, T, NSA_WIDTH)

    q_lat, q_rope, c, kr = mla
    kpos = jnp.arange(T)

    def mla_blk(s):
        return mla_attend(qslice(q_lat, s), qslice(q_rope, s), c, kr, s + jnp.arange(Q_BLOCK), kpos)

    o_mla = mla_up(sweep_query_blocks(mla_blk, T), w_uv)
    new_state = (kv_cmp, kv_sel, kv_win[:, -min(WINDOW, T):], c, kr, S.astype(g.dtype))
    return o_hg, o_nsa, o_mla, new_state


def mixer_sample(hgrn, nsa, mla, hg_gain, w_uv, cmp_pool, sel_pool, lat_pool, kr_pool, win_buf, hg_state, page_table):
    q, log_f, k, v, g = hgrn
    B, T = g.shape[:2]
    past = page_table.shape[1] * PAGE_SIZE
    qpos = past + jnp.arange(T)
    o, S = hgrn2_chunked(q, log_f, k, v, hg_state.astype(jnp.float32))
    o_hg = hgrn_readout(o, g, hg_gain)

    nq, ng, kv_cmp, kv_sel, kv_win = nsa
    kc, vc, ks, vs = nsa_block_keys(
        jnp.concatenate([gather_pages(cmp_pool, page_table), kv_cmp], axis=1),
        jnp.concatenate([gather_pages(sel_pool, page_table), kv_sel], axis=1))
    w_all = jnp.concatenate([win_buf, kv_win], axis=1)
    wpos = past - win_buf.shape[1] + jnp.arange(w_all.shape[1])
    o_nsa = nsa_attend(nq, ng, qpos, kc, vc, ks, vs, w_all[:, :, 0], w_all[:, :, 1], wpos).reshape(B, T, NSA_WIDTH)

    q_lat, q_rope, c, kr = mla
    c_all = jnp.concatenate([gather_pages(lat_pool, page_table), c], axis=1)
    kr_all = jnp.concatenate([gather_pages(kr_pool, page_table), kr], axis=1)
    o_mla = mla_up(mla_attend(q_lat, q_rope, c_all, kr_all, qpos, jnp.arange(past + T)), w_uv)
    new_state = (kv_cmp, kv_sel, w_all[:, -win_buf.shape[1]:], c, kr, S.astype(hg_state.dtype))
    return o_hg, o_nsa, o_mla, new_state


def trunk_layer(x, pos, run_mixer, norm_g, w_gu, w_dn, w_in, lb, mla_gain, w_uk, w_br, w_out):
    h = sandwich_ffn(x, norm_g[0], norm_g[1], w_gu[0], w_dn[0])
    hgrn, nsa, mla, gate_pre = project_mixer_inputs(rmsnorm(h, norm_g[2]), pos, w_in, lb, mla_gain, w_uk)
    o_hg, o_nsa, o_mla, new_state = run_mixer(hgrn, nsa, mla)
    h = h + rmsnorm(merge_branches(o_hg, o_nsa, o_mla, gate_pre, w_br, w_out), norm_g[3])
    return sandwich_ffn(h, norm_g[4], norm_g[5], w_gu[1], w_dn[1]), new_state


def setup_inputs(seed: int = 0) -> dict:
    key = jax.random.key(seed)
    ks = jax.random.split(key, 24)
    f32 = jnp.float32
    n_pages = PAST_LEN // PAGE_SIZE
    n_used = DEC_BATCH * n_pages
    n_pool = n_used + max(1, n_used // 4)
    w_buf = min(WINDOW, PAST_LEN)

    def nrm(k, shape, s=1.0):
        return s * jax.random.normal(k, shape, f32)

    page_table = jax.random.permutation(ks[0], n_pool)[:n_used].reshape(DEC_BATCH, n_pages).astype(jnp.int32)
    return {
        'x_prompt': nrm(ks[1], (BATCH, SEQ, D_MODEL)),
        'x_sample': nrm(ks[2], (DEC_BATCH, DEC_SEQ, D_MODEL)),
        'cache_cmp_kv': nrm(ks[3], (DEPTH, n_pool, PAGE_SIZE, 2, NSA_HD)),
        'cache_sel_kv': nrm(ks[4], (DEPTH, n_pool, PAGE_SIZE, 2, NSA_HD)),
        'cache_mla_latent': nrm(ks[5], (DEPTH, n_pool, PAGE_SIZE, KV_LORA)),
        'cache_mla_krope': nrm(ks[6], (DEPTH, n_pool, PAGE_SIZE, MLA_ROPE)),
        'cache_win_kv': nrm(ks[7], (DEPTH, DEC_BATCH, w_buf, 2, NSA_HD)),
        'state_hgrn': nrm(ks[8], (DEPTH, DEC_BATCH, HG_HEADS, HG_DK, HG_DV), 0.5),
        'page_table': page_table,
        'norm_g': 1.0 + nrm(ks[9], (DEPTH, 6, D_MODEL), 0.02),
        'w_ffn_gu': nrm(ks[10], (DEPTH, 2, D_MODEL, 2 * D_FF), D_MODEL ** -0.5),
        'w_ffn_dn': nrm(ks[11], (DEPTH, 2, D_FF, D_MODEL), D_FF ** -0.5),
        'w_in': nrm(ks[12], (DEPTH, D_MODEL, D_IN), D_MODEL ** -0.5),
        'hg_lb': nrm(ks[13], (DEPTH, HG_HEADS * HG_DK), 0.5),
        'hg_norm': 1.0 + nrm(ks[14], (DEPTH, HG_DV), 0.02),
        'mla_norm': 1.0 + nrm(ks[15], (DEPTH, KV_LORA), 0.02),
        'w_uk': nrm(ks[16], (DEPTH, KV_LORA, MLA_HEADS, MLA_NOPE), KV_LORA ** -0.5),
        'w_uv': nrm(ks[17], (DEPTH, KV_LORA, MLA_HEADS, MLA_VD), KV_LORA ** -0.5),
        'w_br': nrm(ks[18], (DEPTH, MIX_WIDTH, D_MODEL), (MIX_WIDTH // N_BRANCH) ** -0.5),
        'w_out': nrm(ks[19], (DEPTH, D_MODEL, D_MODEL), D_MODEL ** -0.5),
    }


def reference(x_prompt, x_sample, cache_cmp_kv, cache_sel_kv, cache_mla_latent, cache_mla_krope,
              cache_win_kv, state_hgrn, page_table, norm_g, w_ffn_gu, w_ffn_dn, w_in, hg_lb, hg_norm,
              mla_norm, w_uk, w_uv, w_br, w_out):
    lb_all = hgrn_lower_bounds(hg_lb)
    pos_p = jnp.arange(x_prompt.shape[1])
    pos_s = page_table.shape[1] * PAGE_SIZE + jnp.arange(x_sample.shape[1])
    xp, xs = x_prompt, x_sample
    st_p, st_s = [], []
    for l in range(DEPTH):
        layer_w = (norm_g[l], w_ffn_gu[l], w_ffn_dn[l], w_in[l], lb_all[l], mla_norm[l], w_uk[l], w_br[l], w_out[l])
        run_p = functools.partial(mixer_prompt, hg_gain=hg_norm[l], w_uv=w_uv[l])
        xp, sp = trunk_layer(xp, pos_p, run_p, *layer_w)
        run_s = functools.partial(mixer_sample, hg_gain=hg_norm[l], w_uv=w_uv[l],
                                  cmp_pool=cache_cmp_kv[l], sel_pool=cache_sel_kv[l],
                                  lat_pool=cache_mla_latent[l], kr_pool=cache_mla_krope[l],
                                  win_buf=cache_win_kv[l], hg_state=state_hgrn[l], page_table=page_table)
        xs, ss = trunk_layer(xs, pos_s, run_s, *layer_w)
        st_p.append(sp)
        st_s.append(ss)

    def stack(states, i):
        return jnp.stack([s[i] for s in states])

    return (xp, xs,
            stack(st_p, 0), stack(st_p, 1), stack(st_p, 2), stack(st_p, 3), stack(st_p, 4), stack(st_p, 5),
            stack(st_s, 0), stack(st_s, 1), stack(st_s, 2), stack(st_s, 3), stack(st_s, 4), stack(st_s, 5))
```

```python
import functools

import numpy as np
import jax
import jax.numpy as jnp
from jax import lax
from jax.experimental import pallas as pl
from jax.experimental.pallas import tpu as pltpu

F32 = jnp.float32
BF16 = jnp.bfloat16

PAGE_SIZE = 128
HG_HEADS = 4
HG_DK = 128
HG_DV = 128
HG_WIDTH = HG_HEADS * HG_DV
NSA_HEADS = 8
NSA_HD = 64
NSA_WIDTH = NSA_HEADS * NSA_HD
CMP_BLOCK = 32
SEL_BLOCK = 64
N_SEL = 16
WINDOW = 512
FORCED_IMPORTANCE = 1e4
MLA_HEADS = 8
MLA_NOPE = 64
MLA_ROPE = 32
MLA_VD = 64
KV_LORA = 256
MLA_WIDTH = MLA_HEADS * MLA_VD
ROPE_THETA = 500000.0
NSA_ROT = NSA_HD // 4
N_BRANCH = 3
Q_BLOCK = 128
EPS = 1e-6
NEG_INF = -1e30
ABSENT = -3e38

LANES = 128
SUBLANES = 8
VMEM_LIMIT_BYTES = 56 * 1024 * 1024

PJ_NQ = 0
PJ_CMP = PJ_NQ + NSA_WIDTH
PJ_SEL = PJ_CMP + 2 * NSA_HD
PJ_WIN = PJ_SEL + 2 * NSA_HD
PJ_GATE = PJ_WIN + 2 * NSA_HD
PJ_MQN = PJ_GATE + LANES
PJ_MQR = PJ_MQN + MLA_HEADS * MLA_NOPE
PJ_MC = PJ_MQR + MLA_HEADS * MLA_ROPE
PJ_MKR = PJ_MC + KV_LORA
PJ_END = PJ_MKR + LANES


def _params(*sem):
    return pltpu.CompilerParams(dimension_semantics=sem, vmem_limit_bytes=VMEM_LIMIT_BYTES)


def _dot(a, b):
    return jnp.dot(a, b, preferred_element_type=F32)


def _dot_nt(a, b):
    return lax.dot_general(a, b, (((1,), (1,)), ((), ())), preferred_element_type=F32)


def _dot_exact(a, b):
    return jnp.dot(a, b, preferred_element_type=F32, precision=lax.Precision.HIGHEST)


def _rms(x, g):
    return x * lax.rsqrt(jnp.mean(x * x, axis=-1, keepdims=True) + EPS) * g


def _silu(x):
    return x * jax.nn.sigmoid(x)


def _iota(shape, dim):
    return lax.broadcasted_iota(jnp.int32, shape, dim)


def _ffn_body(x_ref, gpre_ref, gpost_ref, wg_ref, wu_ref, wd_ref, o_ref, xn_ref, acc_ref):
    f = pl.program_id(1)

    @pl.when(f == 0)
    def _():
        xn_ref[...] = _rms(x_ref[...], gpre_ref[...]).astype(BF16)
        acc_ref[...] = jnp.zeros_like(acc_ref)

    xn = xn_ref[...]
    a = _dot(xn, wg_ref[...])
    b = _dot(xn, wu_ref[...])
    h = (_silu(a) * b).astype(BF16)
    acc_ref[...] += _dot(h, wd_ref[...])

    @pl.when(f == pl.num_programs(1) - 1)
    def _():
        o_ref[...] = x_ref[...] + 0.5 * _rms(acc_ref[...], gpost_ref[...])


def _ffn(x, norm_g, w_gu, w_dn, layer, which, *, tm=512, tf=512):
    n, d = x.shape
    ff = w_dn.shape[2]
    tm = min(tm, n)
    tf = min(tf, ff)
    nf = ff // tf
    g_pre, g_post = (0, 1) if which == 0 else (4, 5)
    return pl.pallas_call(
        _ffn_body,
        grid=(n // tm, nf),
        in_specs=[
            pl.BlockSpec((tm, d), lambda i, f: (i, 0)),
            pl.BlockSpec((None, None, 1, d), lambda i, f: (layer, g_pre, 0, 0)),
            pl.BlockSpec((None, None, 1, d), lambda i, f: (layer, g_post, 0, 0)),
            pl.BlockSpec((None, None, d, tf), lambda i, f: (layer, which, 0, f)),
            pl.BlockSpec((None, None, d, tf), lambda i, f: (layer, which, 0, f + nf)),
            pl.BlockSpec((None, None, tf, d), lambda i, f: (layer, which, f, 0)),
        ],
        out_specs=pl.BlockSpec((tm, d), lambda i, f: (i, 0)),
        out_shape=jax.ShapeDtypeStruct((n, d), F32),
        scratch_shapes=[pltpu.VMEM((tm, d), BF16), pltpu.VMEM((tm, d), F32)],
        compiler_params=_params("parallel", "arbitrary"),
        name="ffn",
    )(x, norm_g, norm_g, w_gu, w_gu, w_dn)


def _proj_hg_body(x_ref, g_ref, w_ref, o_ref, z_ref):
    @pl.when(pl.program_id(1) == 0)
    def _():
        z_ref[...] = _rms(x_ref[...], g_ref[...]).astype(BF16)

    o_ref[...] = _dot(z_ref[...], w_ref[...])


def _proj_hg(x, norm_g, w_hg, layer, *, tm=512, tn=512):
    n, d = x.shape
    wn = w_hg.shape[2]
    tm = min(tm, n)
    return pl.pallas_call(
        _proj_hg_body,
        grid=(n // tm, wn // tn),
        in_specs=[
            pl.BlockSpec((tm, d), lambda i, j: (i, 0)),
            pl.BlockSpec((None, None, 1, d), lambda i, j: (layer, 2, 0, 0)),
            pl.BlockSpec((None, d, tn), lambda i, j: (layer, 0, j)),
        ],
        out_specs=pl.BlockSpec((tm, tn), lambda i, j: (i, j)),
        out_shape=jax.ShapeDtypeStruct((n, wn), F32),
        scratch_shapes=[pltpu.VMEM((tm, d), BF16)],
        compiler_params=_params("parallel", "arbitrary"),
        name="proj_hg",
    )(x, norm_g, w_hg)


def _rope_lanes(x, tab_ref, base, shift):
    cos, sa, sb = tab_ref[base], tab_ref[base + 1], tab_ref[base + 2]
    return (x * cos + pltpu.roll(x, LANES - shift, 1) * sa + pltpu.roll(x, shift, 1) * sb)


def _proj_nm_body(x_ref, g_ref, w_ref, tab_ref, mg_ref,
                  nq_ref, gate_ref, cmp_ref, sel_ref, win_ref, mqn_ref, mqr_ref, c_ref, kr_ref):
    z = _rms(x_ref[...], g_ref[...]).astype(BF16)
    y = _dot(z, w_ref[...])
    nh = NSA_ROT // 2
    mh = MLA_ROPE // 2
    for c in range(NSA_WIDTH // LANES):
        nq_ref[:, c * LANES:(c + 1) * LANES] = _rope_lanes(
            y[:, PJ_NQ + c * LANES:PJ_NQ + (c + 1) * LANES], tab_ref, 0, nh)
    cmp_ref[...] = _rope_lanes(y[:, PJ_CMP:PJ_CMP + LANES], tab_ref, 3, nh)
    sel_ref[...] = _rope_lanes(y[:, PJ_SEL:PJ_SEL + LANES], tab_ref, 3, nh)
    win_ref[...] = _rope_lanes(y[:, PJ_WIN:PJ_WIN + LANES], tab_ref, 3, nh)
    gate_ref[...] = jax.nn.sigmoid(y[:, PJ_GATE:PJ_GATE + LANES])
    mqn_ref[...] = y[:, PJ_MQN:PJ_MQR]
    for c in range(MLA_HEADS * MLA_ROPE // LANES):
        mqr_ref[:, c * LANES:(c + 1) * LANES] = _rope_lanes(
            y[:, PJ_MQR + c * LANES:PJ_MQR + (c + 1) * LANES], tab_ref, 6, mh)
    c_ref[...] = _rms(y[:, PJ_MC:PJ_MKR], mg_ref[...])
    kr_ref[...] = _rope_lanes(y[:, PJ_MKR:PJ_END], tab_ref, 6, mh)[:, :MLA_ROPE]


def _proj_nm(x, norm_g, w_nm, tabs, mla_norm, layer, *, tm=256):
    n, d = x.shape
    tm = min(tm, n)
    row = lambda i: (i, 0)
    widths = (NSA_WIDTH, LANES, LANES, LANES, LANES, MLA_HEADS * MLA_NOPE, MLA_HEADS * MLA_ROPE,
              KV_LORA, MLA_ROPE)
    return pl.pallas_call(
        _proj_nm_body,
        grid=(n // tm,),
        in_specs=[
            pl.BlockSpec((tm, d), row),
            pl.BlockSpec((None, None, 1, d), lambda i: (layer, 2, 0, 0)),
            pl.BlockSpec((None, d, PJ_END), lambda i: (layer, 0, 0)),
            pl.BlockSpec((9, tm, LANES), lambda i: (0, i, 0)),
            pl.BlockSpec((None, 1, KV_LORA), lambda i: (layer, 0, 0)),
        ],
        out_specs=[pl.BlockSpec((tm, w), row) for w in widths],
        out_shape=[jax.ShapeDtypeStruct((n, w), F32) for w in widths],
        compiler_params=_params("parallel"),
        name="proj_nsa_mla",
    )(x, norm_g, w_nm, tabs, mla_norm)


def _absorb_body(mqn_ref, wuk_ref, o_ref, *, scale):
    for h in range(MLA_HEADS):
        q = mqn_ref[:, h * MLA_NOPE:(h + 1) * MLA_NOPE].astype(BF16)
        o_ref[:, h * KV_LORA:(h + 1) * KV_LORA] = _dot(q, wuk_ref[h]) * scale


def _mla_absorb(mqn, wuk_t, layer, *, tm=512):
    n = mqn.shape[0]
    tm = min(tm, n)
    scale = (MLA_NOPE + MLA_ROPE) ** -0.5
    return pl.pallas_call(
        functools.partial(_absorb_body, scale=scale),
        grid=(n // tm,),
        in_specs=[
            pl.BlockSpec((tm, MLA_HEADS * MLA_NOPE), lambda i: (i, 0)),
            pl.BlockSpec((None, MLA_HEADS, MLA_NOPE, KV_LORA), lambda i: (layer, 0, 0, 0)),
        ],
        out_specs=pl.BlockSpec((tm, MLA_HEADS * KV_LORA), lambda i: (i, 0)),
        out_shape=jax.ShapeDtypeStruct((n, MLA_HEADS * KV_LORA), F32),
        compiler_params=_params("parallel"),
        name="mla_absorb",
    )(mqn, wuk_t)


def _merge_body(x_ref, g_ref, o0_ref, o1_ref, o2_ref, wg0_ref, wg1_ref, wg2_ref,
                wb0_ref, wb1_ref, wb2_ref, y_ref, z_ref):
    @pl.when(pl.program_id(1) == 0)
    def _():
        z_ref[...] = _rms(x_ref[...], g_ref[...]).astype(BF16)

    z = z_ref[...]
    y = jnp.zeros(y_ref.shape, F32)
    for o_ref, wg_ref, wb_ref in ((o0_ref, wg0_ref, wb0_ref), (o1_ref, wg1_ref, wb1_ref),
                                  (o2_ref, wg2_ref, wb2_ref)):
        gate = jax.nn.sigmoid(_dot(z, wg_ref[...]))
        y = y + gate * _dot(o_ref[...].astype(BF16), wb_ref[...])
    y_ref[...] = y.astype(BF16)


def _merge(x, norm_g, o_hg, o_nsa, o_mla, w_gate, w_br, layer, *, tm=512, tn=512):
    n, d = x.shape
    tm = min(tm, n)
    nt = d // tn
    wb = w_br.shape[1] // N_BRANCH
    row = lambda i, j: (i, 0)
    return pl.pallas_call(
        _merge_body,
        grid=(n // tm, nt),
        in_specs=[
            pl.BlockSpec((tm, d), row),
            pl.BlockSpec((None, None, 1, d), lambda i, j: (layer, 2, 0, 0)),
            pl.BlockSpec((tm, wb), row), pl.BlockSpec((tm, wb), row), pl.BlockSpec((tm, wb), row),
            pl.BlockSpec((None, d, tn), lambda i, j: (layer, 0, j)),
            pl.BlockSpec((None, d, tn), lambda i, j: (layer, 0, nt + j)),
            pl.BlockSpec((None, d, tn), lambda i, j: (layer, 0, 2 * nt + j)),
            pl.BlockSpec((None, wb, tn), lambda i, j: (layer, 0, j)),
            pl.BlockSpec((None, wb, tn), lambda i, j: (layer, 1, j)),
            pl.BlockSpec((None, wb, tn), lambda i, j: (layer, 2, j)),
        ],
        out_specs=pl.BlockSpec((tm, tn), lambda i, j: (i, j)),
        out_shape=jax.ShapeDtypeStruct((n, d), BF16),
        scratch_shapes=[pltpu.VMEM((tm, d), BF16)],
        compiler_params=_params("parallel", "arbitrary"),
        name="merge",
    )(x, norm_g, o_hg, o_nsa, o_mla, w_gate, w_gate, w_gate, w_br, w_br, w_br)


def _out_proj_body(x_ref, y_ref, g_ref, w_ref, o_ref):
    o_ref[...] = x_ref[...] + _rms(_dot(y_ref[...], w_ref[...]), g_ref[...])


def _out_proj(x, y, norm_g, w_out, layer, *, tm=512):
    n, d = x.shape
    tm = min(tm, n)
    row = lambda i: (i, 0)
    return pl.pallas_call(
        _out_proj_body,
        grid=(n // tm,),
        in_specs=[
            pl.BlockSpec((tm, d), row),
            pl.BlockSpec((tm, d), row),
            pl.BlockSpec((None, None, 1, d), lambda i: (layer, 3, 0, 0)),
            pl.BlockSpec((None, d, d), lambda i: (layer, 0, 0)),
        ],
        out_specs=pl.BlockSpec((tm, d), row),
        out_shape=jax.ShapeDtypeStruct((n, d), F32),
        compiler_params=_params("parallel"),
        name="out_proj",
    )(x, y, norm_g, w_out)


HG_ROWS = 128
HG_DIAG = SUBLANES


def _lb_body(p_ref, o_ref):
    p = p_ref[...]
    e = jnp.exp(p - jnp.max(p, axis=0, keepdims=True))
    sm = e / jnp.sum(e, axis=0, keepdims=True)
    run = jnp.zeros_like(sm[0:1])
    for l in range(p.shape[0]):
        run = run + sm[l:l + 1]
        o_ref[l:l + 1, :] = run - sm[0:1]


def _hgrn_lower_bounds(hg_lb):
    return pl.pallas_call(
        _lb_body,
        out_shape=jax.ShapeDtypeStruct(hg_lb.shape, F32),
        name="hgrn_lower_bounds",
    )(hg_lb)


def _hg_inputs(x, lb, h):
    w = HG_DK
    hq = x[:, h * w:(h + 1) * w]
    hf = x[:, HG_WIDTH + h * w:HG_WIDTH + (h + 1) * w]
    v = x[:, 2 * HG_WIDTH + h * w:2 * HG_WIDTH + (h + 1) * w]
    g = x[:, 3 * HG_WIDTH + h * w:3 * HG_WIDTH + (h + 1) * w]
    lbh = lb[:, h * w:(h + 1) * w]
    q = _silu(hq)
    log_f = jnp.log(lbh + (1.0 - lbh) * jax.nn.sigmoid(hf))
    k = (1.0 - lbh) * jax.nn.sigmoid(-hf)
    return q, log_f, k, v, g


def _group_last(b, size):
    n = b.shape[0] // size
    b3 = b.reshape(n, size, b.shape[1])
    return jnp.broadcast_to(b3[:, size - 1:size, :], b3.shape).reshape(b.shape)


def _group_mid(b, half):
    n = b.shape[0] // (2 * half)
    b3 = b.reshape(n, 2 * half, b.shape[1])
    return jnp.broadcast_to(b3[:, half - 1:half, :], b3.shape).reshape(b.shape)


def _diag_blocks(q, k, b, rows, cols):
    a = jnp.zeros((q.shape[0], q.shape[0]), F32)
    rin = rows % HG_DIAG
    for d in range(HG_DIAG):
        if d == 0:
            kd, bd = k, b
        else:
            kd, bd = pltpu.roll(k, d, 0), pltpu.roll(b, d, 0)
        w = jnp.sum(q * kd * jnp.exp(jnp.minimum(b - bd, 0.0)), axis=-1, keepdims=True)
        a = a + jnp.where((cols == rows - d) & (rin >= d), w, 0.0)
    return a


def _cross_blocks(q, k, b, rows, cols, top):
    a = jnp.zeros((q.shape[0], q.shape[0]), F32)
    half = top // 2
    while half >= HG_DIAG:
        mid = _group_mid(b, half)
        right = (rows % (2 * half)) >= half
        e = jnp.exp(-jnp.abs(b - mid))
        lf = jnp.where(right, q * e, 0.0).astype(BF16)
        rf = jnp.where(right, 0.0, k * e).astype(BF16)
        same = (rows // (2 * half)) == (cols // (2 * half))
        a = a + jnp.where(same, _dot_nt(lf, rf), 0.0)
        half //= 2
    return a


def _col_of_row(e_row, rows, cols):
    return jnp.sum(jnp.where(rows == cols, e_row, 0.0), axis=-1, keepdims=True)


def _hg_readout(o, g, gain):
    return _rms(o, gain) * _silu(g)


def _hgrn_prompt_body(x_ref, lb_ref, gain_ref, o_ref, s_out_ref, s_ref):
    c = pl.program_id(1)

    @pl.when(c == 0)
    def _():
        s_ref[...] = jnp.zeros_like(s_ref)

    n = HG_ROWS
    rows, cols = _iota((n, n), 0), _iota((n, n), 1)
    tri = (cols <= rows).astype(F32)
    x = x_ref[...]
    lb = lb_ref[...]
    for h in range(HG_HEADS):
        q, log_f, k, v, g = _hg_inputs(x, lb, h)
        b = _dot_exact(tri, log_f)
        a = _diag_blocks(q, k, b, rows, cols) + _cross_blocks(q, k, b, rows, cols, n)
        s = s_ref[h]
        vb = v.astype(BF16)
        o = _dot((q * jnp.exp(b)).astype(BF16), s.astype(BF16)) + _dot(a.astype(BF16), vb)
        b_last = b[n - 1:n, :]
        kp = k * jnp.exp(b_last - b)
        s_new = _col_of_row(jnp.exp(b_last), rows, cols) * s + _dot(kp.T.astype(BF16), vb)
        s_ref[h] = s_new
        o_ref[:, h * HG_DV:(h + 1) * HG_DV] = _hg_readout(o, g, gain_ref[...])

    @pl.when(c == pl.num_programs(1) - 1)
    def _():
        s_out_ref[...] = s_ref[...]


def _hgrn_prompt(hg_raw, lb, gain, layer, batch, seq):
    nc = seq // HG_ROWS
    return pl.pallas_call(
        _hgrn_prompt_body,
        grid=(batch, nc),
        in_specs=[
            pl.BlockSpec((HG_ROWS, 4 * HG_WIDTH), lambda b, c: (b * nc + c, 0)),
            pl.BlockSpec((None, 1, HG_WIDTH), lambda b, c: (layer, 0, 0)),
            pl.BlockSpec((None, 1, HG_DV), lambda b, c: (layer, 0, 0)),
        ],
        out_specs=[
            pl.BlockSpec((HG_ROWS, HG_WIDTH), lambda b, c: (b * nc + c, 0)),
            pl.BlockSpec((None, HG_HEADS, HG_DK, HG_DV), lambda b, c: (b, 0, 0, 0)),
        ],
        out_shape=[jax.ShapeDtypeStruct((batch * seq, HG_WIDTH), F32),
                   jax.ShapeDtypeStruct((batch, HG_HEADS, HG_DK, HG_DV), F32)],
        scratch_shapes=[pltpu.VMEM((HG_HEADS, HG_DK, HG_DV), F32)],
        compiler_params=_params("parallel", "arbitrary"),
        name="hgrn_prompt",
    )(hg_raw, lb, gain)


def _hgrn_sample_body(x_ref, lb_ref, gain_ref, s0_ref, o_ref, s_out_ref, *, t):
    n = HG_ROWS
    nseq = n // t
    rows, cols = _iota((n, n), 0), _iota((n, n), 1)
    tri = ((cols <= rows) & (rows // t == cols // t)).astype(F32)
    x = x_ref[...]
    lb = lb_ref[...]
    for h in range(HG_HEADS):
        q, log_f, k, v, g = _hg_inputs(x, lb, h)
        b = _dot_exact(tri, log_f)
        a = _diag_blocks(q, k, b, rows, cols)
        vb = v.astype(BF16)
        o = _dot(a.astype(BF16), vb)
        eb = jnp.exp(b)
        qe = q * eb
        b_last = _group_last(b, t)
        kpt = (k * jnp.exp(b_last - b)).T
        for i in range(nseq):
            mine = (rows // t) == i
            s0 = s0_ref[i, h]
            o = o + _dot(jnp.where(mine, qe, 0.0).astype(BF16), s0.astype(BF16))
            e_col = _col_of_row(eb[i * t + t - 1:i * t + t, :], rows, cols)
            upd = _dot(jnp.where((cols // t) == i, kpt, 0.0).astype(BF16), vb)
            s_out_ref[i, h] = e_col * s0 + upd
        o_ref[:, h * HG_DV:(h + 1) * HG_DV] = _hg_readout(o, g, gain_ref[...])


def _hgrn_sample(hg_raw, lb, gain, state, layer, row0, batch, t):
    assert t == HG_DIAG and HG_ROWS % t == 0
    nseq = HG_ROWS // t
    blk0 = row0 // HG_ROWS
    return pl.pallas_call(
        functools.partial(_hgrn_sample_body, t=t),
        grid=(batch // nseq,),
        in_specs=[
            pl.BlockSpec((HG_ROWS, 4 * HG_WIDTH), lambda i: (blk0 + i, 0)),
            pl.BlockSpec((None, 1, HG_WIDTH), lambda i: (layer, 0, 0)),
            pl.BlockSpec((None, 1, HG_DV), lambda i: (layer, 0, 0)),
            pl.BlockSpec((None, nseq, HG_HEADS, HG_DK, HG_DV), lambda i: (layer, i, 0, 0, 0)),
        ],
        out_specs=[
            pl.BlockSpec((HG_ROWS, HG_WIDTH), lambda i: (i, 0)),
            pl.BlockSpec((nseq, HG_HEADS, HG_DK, HG_DV), lambda i: (i, 0, 0, 0)),
        ],
        out_shape=[jax.ShapeDtypeStruct((batch * t, HG_WIDTH), F32),
                   jax.ShapeDtypeStruct((batch, HG_HEADS, HG_DK, HG_DV), F32)],
        compiler_params=_params("parallel"),
        name="hgrn_sample",
    )(hg_raw, lb, gain, state)


def _stack_heads(x, heads, width):
    return jnp.concatenate([x[:, h * width:(h + 1) * width] for h in range(heads)], axis=0)


def _mla_up(o_lat, wuv_ref, o_ref, t):
    for h in range(MLA_HEADS):
        ol = o_lat[h * t:(h + 1) * t, :].astype(BF16)
        o_ref[:, h * MLA_VD:(h + 1) * MLA_VD] = _dot(ol, wuv_ref[h])


def _mla_prompt_body(ql_in_ref, qr_in_ref, c_ref, kr_ref, wuv_ref, o_ref,
                     ql_ref, qr_ref, m_ref, l_ref, acc_ref, *, tq, tk, scale):
    i, j = pl.program_id(1), pl.program_id(2)
    q0, k0 = i * tq, j * tk
    last_j = (q0 + tq - 1) // tk

    @pl.when(j == 0)
    def _():
        ql_ref[...] = _stack_heads(ql_in_ref[...], MLA_HEADS, KV_LORA).astype(BF16)
        qr_ref[...] = (_stack_heads(qr_in_ref[...], MLA_HEADS, MLA_ROPE) * scale).astype(BF16)
        m_ref[...] = jnp.full_like(m_ref, NEG_INF)
        l_ref[...] = jnp.zeros_like(l_ref)
        acc_ref[...] = jnp.zeros_like(acc_ref)

    @pl.when(j <= last_j)
    def _():
        cb = c_ref[...].astype(BF16)
        kb = kr_ref[...].astype(BF16)
        s = _dot_nt(ql_ref[...], cb) + _dot_nt(qr_ref[...], kb)
        qpos = q0 + (_iota(s.shape, 0) & (tq - 1))
        kpos = k0 + _iota(s.shape, 1)
        s = jnp.where(kpos <= qpos, s, NEG_INF)
        m_prev = m_ref[...]
        m_new = jnp.maximum(m_prev, jnp.max(s, axis=-1, keepdims=True))
        alpha = jnp.exp(m_prev - m_new)
        p = jnp.exp(s - m_new)
        l_ref[...] = alpha * l_ref[...] + jnp.sum(p, axis=-1, keepdims=True)
        acc_ref[...] = alpha * acc_ref[...] + _dot(p.astype(BF16), cb)
        m_ref[...] = m_new

    @pl.when(j == last_j)
    def _():
        _mla_up(acc_ref[...] / l_ref[...], wuv_ref, o_ref, tq)


def _mla_prompt(q_lat, mqr, c, kr, wuv, layer, batch, seq, *, tq=128, tk=512):
    tk = min(tk, seq)
    nq, nk = seq // tq, seq // tk
    scale = (MLA_NOPE + MLA_ROPE) ** -0.5
    hq = MLA_HEADS * tq

    def kv_map(b, i, j):
        return (b * nk + jnp.minimum(j, (i * tq + tq - 1) // tk), 0)

    return pl.pallas_call(
        functools.partial(_mla_prompt_body, tq=tq, tk=tk, scale=scale),
        grid=(batch, nq, nk),
        in_specs=[
            pl.BlockSpec((tq, MLA_HEADS * KV_LORA), lambda b, i, j: (b * nq + i, 0)),
            pl.BlockSpec((tq, MLA_HEADS * MLA_ROPE), lambda b, i, j: (b * nq + i, 0)),
            pl.BlockSpec((tk, KV_LORA), kv_map),
            pl.BlockSpec((tk, MLA_ROPE), kv_map),
            pl.BlockSpec((None, MLA_HEADS, KV_LORA, MLA_VD), lambda b, i, j: (layer, 0, 0, 0)),
        ],
        out_specs=pl.BlockSpec((tq, MLA_WIDTH), lambda b, i, j: (b * nq + i, 0)),
        out_shape=jax.ShapeDtypeStruct((batch * seq, MLA_WIDTH), F32),
        scratch_shapes=[pltpu.VMEM((hq, KV_LORA), BF16), pltpu.VMEM((hq, MLA_ROPE), BF16),
                        pltpu.VMEM((hq, 1), F32), pltpu.VMEM((hq, 1), F32),
                        pltpu.VMEM((hq, KV_LORA), F32)],
        compiler_params=_params("parallel", "parallel", "arbitrary"),
        name="mla_prompt",
    )(q_lat, mqr, c, kr, wuv)


NEW_PAD = 16


def _pad_rows(x, rows):
    return jnp.concatenate([x, jnp.zeros((rows - x.shape[0], x.shape[1]), x.dtype)], axis=0)


def _page_copies(pt_ref, pool_ref, buf_ref, sem_ref, layer, seq, slot, n_pages):
    def copy(p):
        return pltpu.make_async_copy(pool_ref.at[layer, pt_ref[seq, p]],
                                     buf_ref.at[slot, pl.ds(p * PAGE_SIZE, PAGE_SIZE)],
                                     sem_ref.at[slot])
    return copy


def _start_pages(copy, n_pages):
    def body(p, carry):
        copy(p).start()
        return carry
    lax.fori_loop(0, n_pages, body, 0)


def _wait_pages(copy, n_pages):
    def body(p, carry):
        copy(p).wait()
        return carry
    lax.fori_loop(0, n_pages, body, 0)


def _mla_sample_body(pt_ref, ql_in_ref, qr_in_ref, c_new_ref, kr_new_ref, wuv_ref, lat_hbm, kr_hbm,
                     o_ref, lat_buf, kr_buf, s_buf, sems, *, layer, t, n_pages, chunk, scale):
    b = pl.program_id(0)
    nb = pl.num_programs(0)
    slot = b % 2
    past = n_pages * PAGE_SIZE

    def fetch(seq, sl, start):
        for pool, buf, k in ((lat_hbm, lat_buf, 0), (kr_hbm, kr_buf, 1)):
            copy = _page_copies(pt_ref, pool, buf, sems.at[k], layer, seq, sl, n_pages)
            (_start_pages if start else _wait_pages)(copy, n_pages)

    @pl.when(b == 0)
    def _():
        fetch(0, 0, True)

    @pl.when(b + 1 < nb)
    def _():
        fetch(b + 1, 1 - slot, True)

    fetch(b, slot, False)

    ql = _stack_heads(ql_in_ref[...], MLA_HEADS, KV_LORA).astype(BF16)
    qr = (_stack_heads(qr_in_ref[...], MLA_HEADS, MLA_ROPE) * scale).astype(BF16)
    rows = MLA_HEADS * t
    m = jnp.full((rows, 1), NEG_INF, F32)
    for ch in range(past // chunk):
        cb = lat_buf[slot, ch * chunk:(ch + 1) * chunk, :].astype(BF16)
        kb = kr_buf[slot, ch * chunk:(ch + 1) * chunk, :].astype(BF16)
        s = _dot_nt(ql, cb) + _dot_nt(qr, kb)
        s_buf[:, ch * chunk:(ch + 1) * chunk] = s
        m = jnp.maximum(m, jnp.max(s, axis=-1, keepdims=True))
    c_new = _pad_rows(c_new_ref[...], NEW_PAD).astype(BF16)
    kr_new = _pad_rows(kr_new_ref[...], NEW_PAD).astype(BF16)
    s_new = _dot_nt(ql, c_new) + _dot_nt(qr, kr_new)
    tok = _iota(s_new.shape, 0) & (t - 1)
    s_new = jnp.where(_iota(s_new.shape, 1) <= tok, s_new, NEG_INF)
    m = jnp.maximum(m, jnp.max(s_new, axis=-1, keepdims=True))
    p_new = jnp.exp(s_new - m)
    l = jnp.sum(p_new, axis=-1, keepdims=True)
    acc = _dot(p_new.astype(BF16), c_new)
    for ch in range(past // chunk):
        p = jnp.exp(s_buf[:, ch * chunk:(ch + 1) * chunk] - m)
        l = l + jnp.sum(p, axis=-1, keepdims=True)
        cb = lat_buf[slot, ch * chunk:(ch + 1) * chunk, :].astype(BF16)
        acc = acc + _dot(p.astype(BF16), cb)
    _mla_up(acc / l, wuv_ref, o_ref, t)


def _mla_sample(q_lat, mqr, c, kr, wuv, lat_pool, kr_pool, page_table, layer, row0, t, *, chunk=1024):
    batch, n_pages = page_table.shape
    assert t & (t - 1) == 0 and t <= NEW_PAD
    past = n_pages * PAGE_SIZE
    chunk = min(chunk, past)
    blk0 = row0 // t
    rows = MLA_HEADS * t
    scale = (MLA_NOPE + MLA_ROPE) ** -0.5
    tok = lambda i, pt: (blk0 + i, 0)
    grid_spec = pltpu.PrefetchScalarGridSpec(
        num_scalar_prefetch=1,
        grid=(batch,),
        in_specs=[
            pl.BlockSpec((t, MLA_HEADS * KV_LORA), tok),
            pl.BlockSpec((t, MLA_HEADS * MLA_ROPE), tok),
            pl.BlockSpec((t, KV_LORA), tok),
            pl.BlockSpec((t, MLA_ROPE), tok),
            pl.BlockSpec((None, MLA_HEADS, KV_LORA, MLA_VD), lambda i, pt: (layer, 0, 0, 0)),
            pl.BlockSpec(memory_space=pl.ANY),
            pl.BlockSpec(memory_space=pl.ANY),
        ],
        out_specs=pl.BlockSpec((t, MLA_WIDTH), lambda i, pt: (i, 0)),
        scratch_shapes=[pltpu.VMEM((2, past, KV_LORA), F32), pltpu.VMEM((2, past, MLA_ROPE), F32),
                        pltpu.VMEM((rows, past), F32), pltpu.SemaphoreType.DMA((2, 2))],
    )
    return pl.pallas_call(
        functools.partial(_mla_sample_body, layer=layer, t=t, n_pages=n_pages, chunk=chunk, scale=scale),
        grid_spec=grid_spec,
        out_shape=jax.ShapeDtypeStruct((batch * t, MLA_WIDTH), F32),
        compiler_params=_params("arbitrary"),
        name="mla_sample",
    )(page_table, q_lat, mqr, c, kr, wuv, lat_pool, kr_pool)


def _block_means(x):
    pairs = x.reshape(x.shape[0] // SEL_BLOCK, SEL_BLOCK, x.shape[1])
    even = jnp.sum(pairs[:, :CMP_BLOCK, :], axis=1)
    odd = jnp.sum(pairs[:, CMP_BLOCK:, :], axis=1)
    return jnp.concatenate([even, odd], axis=0) / CMP_BLOCK


def _means_body(x_ref, o_ref):
    o_ref[...] = _block_means(x_ref[...])


def _nsa_means(kv_cmp, batch, seq):
    return pl.pallas_call(
        _means_body,
        grid=(batch,),
        in_specs=[pl.BlockSpec((seq, LANES), lambda b: (b, 0))],
        out_specs=pl.BlockSpec((None, seq // CMP_BLOCK, LANES), lambda b: (b, 0, 0)),
        out_shape=jax.ShapeDtypeStruct((batch, seq // CMP_BLOCK, LANES), F32),
        compiler_params=_params("parallel"),
        name="nsa_means",
    )(kv_cmp)


def _softmax_rows(s, mask):
    if mask is not None:
        s = jnp.where(mask, s, NEG_INF)
    m = jnp.max(s, axis=-1, keepdims=True)
    e = jnp.exp(s - m)
    if mask is not None:
        e = jnp.where(mask, e, 0.0)
    l = jnp.sum(e, axis=-1, keepdims=True)
    return e / jnp.where(l > 0.0, l, 1.0)


def _select_blocks(imp, qpos, ns):
    blk = _iota(imp.shape, 1)
    cur = qpos // SEL_BLOCK
    forced = (blk == 0) | (blk == cur) | (blk == cur - 1)
    key = jnp.where(forced, imp + FORCED_IMPORTANCE, imp)
    key = jnp.where(blk * SEL_BLOCK <= qpos, key, NEG_INF)
    key = jnp.where(blk < ns, key, ABSENT)
    cnt = jnp.zeros(imp.shape, F32)
    for i in range(ns):
        col = key[:, i:i + 1]
        beats = (col > key) | ((col == key) & (blk > i))
        cnt = cnt + jnp.where(beats, 1.0, 0.0)
    return cnt < float(min(N_SEL, ns))


def _pad_lanes(x, lanes):
    if x.shape[1] == lanes:
        return x
    return jnp.concatenate([x, jnp.zeros((x.shape[0], lanes - x.shape[1]), x.dtype)], axis=1)


def _nsa_combine(gate, o_c, o_s, o_w, o_ref, t):
    for h in range(NSA_HEADS):
        r = slice(h * t, (h + 1) * t)
        o_ref[:, h * NSA_HD:(h + 1) * NSA_HD] = (gate[:, 3 * h:3 * h + 1] * o_c[r]
                                                 + gate[:, 3 * h + 1:3 * h + 2] * o_s[r]
                                                 + gate[:, 3 * h + 2:3 * h + 3] * o_w[r])


def _nsa_prompt_body(nq_ref, gate_ref, mean_ref, sel_ref, win_ref, exp_ref, o_ref, *, tq, tk, seq, wlen):
    i = pl.program_id(1)
    q0 = i * tq
    scale = NSA_HD ** -0.5
    rows = NSA_HEADS * tq
    q = (_stack_heads(nq_ref[...], NSA_HEADS, NSA_HD) * scale).astype(BF16)
    tok_pos = q0 + (_iota((rows, 1), 0) & (tq - 1))
    qpos = q0 + _iota((tq, 1), 0)

    nc = seq // CMP_BLOCK
    ns = seq // SEL_BLOCK
    means = mean_ref[...]
    kc, vc = means[:, :NSA_HD].astype(BF16), means[:, NSA_HD:].astype(BF16)
    s_c = _dot_nt(q, kc)
    col = _iota(s_c.shape, 1)
    cid = jnp.where(col < ns, 2 * col, 2 * (col - ns) + 1)
    p_c = _softmax_rows(s_c, (cid + 1) * CMP_BLOCK - 1 <= tok_pos)
    o_c = _dot(p_c.astype(BF16), vc)
    imp = jnp.sum(p_c.reshape(NSA_HEADS, tq, nc), axis=0)
    lanes = exp_ref.shape[1]
    imp = _pad_lanes(imp[:, :ns] + imp[:, ns:], lanes)
    picked = jnp.where(_select_blocks(imp, qpos, ns), 1.0, 0.0).astype(BF16)

    def kv_step(j, carry):
        m_prev, l_prev, acc = carry
        k0 = pl.multiple_of(j * tk, tk)
        kv = sel_ref[pl.ds(k0, tk), :]
        ks, vs = kv[:, :NSA_HD].astype(BF16), kv[:, NSA_HD:].astype(BF16)
        s = _dot_nt(q, ks)
        ok = (_dot(picked, exp_ref[j]) > 0.5) & (k0 + _iota((tq, tk), 1) <= qpos)
        s = jnp.where(ok[None], s.reshape(NSA_HEADS, tq, tk), NEG_INF).reshape(rows, tk)
        m_new = jnp.maximum(m_prev, jnp.max(s, axis=-1, keepdims=True))
        alpha = jnp.exp(m_prev - m_new)
        p = jnp.exp(s - m_new)
        l_new = alpha * l_prev + jnp.sum(p, axis=-1, keepdims=True)
        return m_new, l_new, alpha * acc + _dot(p.astype(BF16), vs)

    init = (jnp.full((rows, 1), NEG_INF, F32), jnp.zeros((rows, 1), F32), jnp.zeros((rows, NSA_HD), F32))
    _, l_s, acc_s = lax.fori_loop(0, (q0 + tq + tk - 1) // tk, kv_step, init)
    o_s = acc_s / l_s

    w0 = pl.multiple_of(jnp.clip(q0 + tq - wlen, 0, seq - wlen), SUBLANES)
    wkv = win_ref[pl.ds(w0, wlen), :]
    kw, vw = wkv[:, :NSA_HD].astype(BF16), wkv[:, NSA_HD:].astype(BF16)
    s_w = _dot_nt(q, kw)
    wpos = w0 + _iota(s_w.shape, 1)
    p_w = _softmax_rows(s_w, (wpos <= tok_pos) & (wpos > tok_pos - WINDOW))
    o_w = _dot(p_w.astype(BF16), vw)
    _nsa_combine(gate_ref[...], o_c, o_s, o_w, o_ref, tq)


def _expand_matrix(lanes, keys, tile):
    e = (np.arange(keys)[None, :] // SEL_BLOCK) == np.arange(lanes)[:, None]
    e = e.reshape(lanes, keys // tile, tile).transpose(1, 0, 2)
    return jnp.asarray(e, BF16)


def _nsa_prompt(nq, gates, means, kv_sel, kv_win, batch, seq, *, tq=Q_BLOCK, tk=512):
    tk = min(tk, seq)
    nqb = seq // tq
    wlen = min(WINDOW + tq, seq)
    lanes = -(-(seq // SEL_BLOCK) // LANES) * LANES
    expand = _expand_matrix(lanes, seq, tk)
    tok = lambda b, i: (b * nqb + i, 0)
    return pl.pallas_call(
        functools.partial(_nsa_prompt_body, tq=tq, tk=tk, seq=seq, wlen=wlen),
        grid=(batch, nqb),
        in_specs=[
            pl.BlockSpec((tq, NSA_WIDTH), tok),
            pl.BlockSpec((tq, LANES), tok),
            pl.BlockSpec((None, seq // CMP_BLOCK, LANES), lambda b, i: (b, 0, 0)),
            pl.BlockSpec((seq, LANES), lambda b, i: (b, 0)),
            pl.BlockSpec((seq, LANES), lambda b, i: (b, 0)),
            pl.BlockSpec(expand.shape, lambda b, i: (0, 0, 0)),
        ],
        out_specs=pl.BlockSpec((tq, NSA_WIDTH), tok),
        out_shape=jax.ShapeDtypeStruct((batch * seq, NSA_WIDTH), F32),
        compiler_params=_params("parallel", "parallel"),
        name="nsa_prompt",
    )(nq, gates, means, kv_sel, kv_win, expand)


def _nsa_sample_body(pt_ref, nq_ref, gate_ref, sel_new_ref, win_new_ref, wbuf_ref, exp_ref, cmp_hbm, sel_hbm,
                     o_ref, cmp_buf, sel_buf, s_buf, sems, *, layer, t, n_pages, chunk):
    b = pl.program_id(0)
    nb = pl.num_programs(0)
    slot = b % 2
    past = n_pages * PAGE_SIZE

    def fetch(seq, sl, start):
        for pool, buf, k in ((cmp_hbm, cmp_buf, 0), (sel_hbm, sel_buf, 1)):
            copy = _page_copies(pt_ref, pool, buf, sems.at[k], layer, seq, sl, n_pages)
            (_start_pages if start else _wait_pages)(copy, n_pages)

    @pl.when(b == 0)
    def _():
        fetch(0, 0, True)

    @pl.when(b + 1 < nb)
    def _():
        fetch(b + 1, 1 - slot, True)

    fetch(b, slot, False)

    scale = NSA_HD ** -0.5
    rows = NSA_HEADS * t
    q = (_stack_heads(nq_ref[...], NSA_HEADS, NSA_HD) * scale).astype(BF16)
    tok = _iota((rows, 1), 0) & (t - 1)
    qpos = past + tok

    ncp = past // CMP_BLOCK
    nsp = past // SEL_BLOCK
    ns = nsp + 1
    means = _block_means(cmp_buf[slot])
    kc, vc = means[:, :NSA_HD].astype(BF16), means[:, NSA_HD:].astype(BF16)
    p_c = _softmax_rows(_dot_nt(q, kc), None)
    o_c = _dot(p_c.astype(BF16), vc)
    imp = jnp.sum(p_c.reshape(NSA_HEADS, t, ncp), axis=0)
    imp = jnp.concatenate([imp] * NSA_HEADS, axis=0)
    lanes = exp_ref.shape[1]
    imp = _pad_lanes(imp[:, :nsp] + imp[:, nsp:], lanes)
    chosen = _select_blocks(imp, qpos, ns)
    picked = jnp.where(chosen, 1.0, 0.0).astype(BF16)

    sel_new = _pad_rows(sel_new_ref[...], NEW_PAD)
    s_new = _dot_nt(q, sel_new[:, :NSA_HD].astype(BF16))
    new_ok = (_iota(s_new.shape, 1) <= tok)
    pick_new = jnp.sum(jnp.where(_iota(chosen.shape, 1) == nsp, picked.astype(F32), 0.0),
                       axis=-1, keepdims=True) > 0.5
    s_new = jnp.where(new_ok & pick_new, s_new, NEG_INF)
    m = jnp.max(s_new, axis=-1, keepdims=True)
    for ch in range(past // chunk):
        ks = sel_buf[slot, ch * chunk:(ch + 1) * chunk, :NSA_HD].astype(BF16)
        s = jnp.where(_dot(picked, exp_ref[ch]) > 0.5, _dot_nt(q, ks), NEG_INF)
        s_buf[:, ch * chunk:(ch + 1) * chunk] = s
        m = jnp.maximum(m, jnp.max(s, axis=-1, keepdims=True))
    p_new = jnp.exp(s_new - m)
    l = jnp.sum(p_new, axis=-1, keepdims=True)
    acc = _dot(p_new.astype(BF16), sel_new[:, NSA_HD:].astype(BF16))
    for ch in range(past // chunk):
        p = jnp.exp(s_buf[:, ch * chunk:(ch + 1) * chunk] - m)
        l = l + jnp.sum(p, axis=-1, keepdims=True)
        vs = sel_buf[slot, ch * chunk:(ch + 1) * chunk, NSA_HD:].astype(BF16)
        acc = acc + _dot(p.astype(BF16), vs)
    o_s = acc / l

    wb = wbuf_ref.shape[0]
    wkv = jnp.concatenate([wbuf_ref[...], _pad_rows(win_new_ref[...], NEW_PAD)], axis=0)
    s_w = _dot_nt(q, wkv[:, :NSA_HD].astype(BF16))
    wi = _iota(s_w.shape, 1)
    w_ok = ((wi < wb) & (wi > tok + (wb - WINDOW))) | ((wi >= wb) & (wi - wb <= tok))
    p_w = _softmax_rows(s_w, w_ok)
    o_w = _dot(p_w.astype(BF16), wkv[:, NSA_HD:].astype(BF16))
    _nsa_combine(gate_ref[...], o_c, o_s, o_w, o_ref, t)


def _nsa_sample(nq, gates, kv_sel, kv_win, win_buf, cmp_pool, sel_pool, page_table, layer, row0, t, *, chunk=1024):
    batch, n_pages = page_table.shape
    assert t & (t - 1) == 0 and t <= NEW_PAD and t < CMP_BLOCK
    past = n_pages * PAGE_SIZE
    chunk = min(chunk, past)
    blk0 = row0 // t
    rows = NSA_HEADS * t
    wb = win_buf.shape[2]
    lanes = -(-(past // SEL_BLOCK + 1) // LANES) * LANES
    expand = _expand_matrix(lanes, past, chunk)
    tok = lambda i, pt: (blk0 + i, 0)
    grid_spec = pltpu.PrefetchScalarGridSpec(
        num_scalar_prefetch=1,
        grid=(batch,),
        in_specs=[
            pl.BlockSpec((t, NSA_WIDTH), tok),
            pl.BlockSpec((t, LANES), tok),
            pl.BlockSpec((t, LANES), tok),
            pl.BlockSpec((t, LANES), tok),
            pl.BlockSpec((None, None, wb, LANES), lambda i, pt: (layer, i, 0, 0)),
            pl.BlockSpec(expand.shape, lambda i, pt: (0, 0, 0)),
            pl.BlockSpec(memory_space=pl.ANY),
            pl.BlockSpec(memory_space=pl.ANY),
        ],
        out_specs=pl.BlockSpec((t, NSA_WIDTH), lambda i, pt: (i, 0)),
        scratch_shapes=[pltpu.VMEM((2, past, LANES), F32), pltpu.VMEM((2, past, LANES), F32),
                        pltpu.VMEM((rows, past), F32), pltpu.SemaphoreType.DMA((2, 2))],
    )
    return pl.pallas_call(
        functools.partial(_nsa_sample_body, layer=layer, t=t, n_pages=n_pages, chunk=chunk),
        grid_spec=grid_spec,
        out_shape=jax.ShapeDtypeStruct((batch * t, NSA_WIDTH), F32),
        compiler_params=_params("arbitrary"),
        name="nsa_sample",
    )(page_table, nq, gates, kv_sel, kv_win, win_buf, expand, cmp_pool, sel_pool)


def _pack_w_in(w_in):
    d_model = w_in.shape[1]
    cuts = np.cumsum((HG_HEADS * HG_DK, HG_HEADS * HG_DK, HG_WIDTH, HG_WIDTH, NSA_WIDTH, 2 * NSA_HD,
                      2 * NSA_HD, 2 * NSA_HD, 3 * NSA_HEADS, MLA_HEADS * (MLA_NOPE + MLA_ROPE),
                      KV_LORA, MLA_ROPE, N_BRANCH * d_model)).tolist()
    w = w_in.astype(BF16)
    depth = w.shape[0]
    w_hg = w[:, :, :cuts[3]]
    nsa = w[:, :, cuts[3]:cuts[7]]
    ngate = w[:, :, cuts[7]:cuts[8]]
    mq = w[:, :, cuts[8]:cuts[9]].reshape(depth, d_model, MLA_HEADS, MLA_NOPE + MLA_ROPE)
    mqn = mq[..., :MLA_NOPE].reshape(depth, d_model, MLA_HEADS * MLA_NOPE)
    mqr = mq[..., MLA_NOPE:].reshape(depth, d_model, MLA_HEADS * MLA_ROPE)
    mc = w[:, :, cuts[9]:cuts[10]]
    mkr = w[:, :, cuts[10]:cuts[11]]
    pad = lambda k: jnp.zeros((depth, d_model, k), BF16)
    w_nm = jnp.concatenate([nsa, ngate, pad(LANES - 3 * NSA_HEADS), mqn, mqr, mc, mkr,
                            pad(LANES - MLA_ROPE)], axis=-1)
    w_gate = w[:, :, cuts[11]:]
    return w_hg, w_nm, w_gate


def _rope_tables(pos):
    n = pos.shape[0]
    posf = pos.astype(F32)[:, None]

    def cs(half):
        inv_freq = ROPE_THETA ** (-jnp.arange(half, dtype=F32) / half)
        ang = posf * inv_freq[None, :]
        return jnp.cos(ang), jnp.sin(ang)

    cn, sn = cs(NSA_ROT // 2)
    zn = jnp.zeros_like(sn)
    rest = NSA_HD - NSA_ROT
    head = (jnp.concatenate([cn, cn, jnp.ones((n, rest), F32)], 1),
            jnp.concatenate([-sn, zn, jnp.zeros((n, rest), F32)], 1),
            jnp.concatenate([zn, sn, jnp.zeros((n, rest), F32)], 1))
    ident = (jnp.ones((n, NSA_HD), F32), jnp.zeros((n, NSA_HD), F32), jnp.zeros((n, NSA_HD), F32))
    q_tabs = [jnp.tile(t, (1, LANES // NSA_HD)) for t in head]
    kv_tabs = [jnp.concatenate([t, i], 1) for t, i in zip(head, ident)]
    cm, sm = cs(MLA_ROPE // 2)
    zm = jnp.zeros_like(sm)
    grp = (jnp.concatenate([cm, cm], 1), jnp.concatenate([-sm, zm], 1), jnp.concatenate([zm, sm], 1))
    m_tabs = [jnp.tile(t, (1, LANES // MLA_ROPE)) for t in grp]
    return jnp.stack(q_tabs + kv_tabs + m_tabs)


def kernel(x_prompt, x_sample, cache_cmp_kv, cache_sel_kv, cache_mla_latent, cache_mla_krope, cache_win_kv,
           state_hgrn, page_table, norm_g, w_ffn_gu, w_ffn_dn, w_in, hg_lb, hg_norm, mla_norm, w_uk, w_uv,
           w_br, w_out):
    batch, seq, d = x_prompt.shape
    bs, ts, _ = x_sample.shape
    depth = norm_g.shape[0]
    n_pool = cache_cmp_kv.shape[1]
    n_pages = page_table.shape[1]
    past = n_pages * PAGE_SIZE
    wb = cache_win_kv.shape[2]
    rp = batch * seq
    n = rp + bs * ts
    assert n % 512 == 0 and rp % 512 == 0 and seq % 512 == 0 and bs % (HG_ROWS // ts) == 0

    x = jnp.concatenate([x_prompt.reshape(rp, d), x_sample.reshape(bs * ts, d)], axis=0)
    pos = jnp.concatenate([jnp.tile(jnp.arange(seq, dtype=jnp.int32), batch),
                           jnp.tile(past + jnp.arange(ts, dtype=jnp.int32), bs)])
    tabs = _rope_tables(pos)
    ng = norm_g.reshape(depth, 6, 1, d)
    w_gu = w_ffn_gu.astype(BF16)
    w_dn = w_ffn_dn.astype(BF16)
    w_hg, w_nm, w_gate = _pack_w_in(w_in)
    wuk_t = jnp.transpose(w_uk, (0, 2, 3, 1)).astype(BF16)
    wuv_t = jnp.transpose(w_uv, (0, 2, 1, 3)).astype(BF16)
    w_brb = w_br.astype(BF16)
    w_outb = w_out.astype(BF16)
    lb_all = _hgrn_lower_bounds(hg_lb)[:, None, :]
    hg_gain = hg_norm[:, None, :]
    mla_gain = mla_norm[:, None, :]
    cmp_pool = cache_cmp_kv.reshape(depth, n_pool, PAGE_SIZE, 2 * NSA_HD)
    sel_pool = cache_sel_kv.reshape(depth, n_pool, PAGE_SIZE, 2 * NSA_HD)
    win_buf = cache_win_kv.reshape(depth, bs, wb, 2 * NSA_HD)

    st_p, st_s = [], []
    for l in range(depth):
        x = _ffn(x, ng, w_gu, w_dn, l, 0)
        hg_raw = _proj_hg(x, ng, w_hg, l)
        nq, gates, kv_cmp, kv_sel, kv_win, mqn, mqr, c, kr = _proj_nm(x, ng, w_nm, tabs, mla_gain, l)
        q_lat = _mla_absorb(mqn, wuk_t, l)

        o_hg_p, s_p = _hgrn_prompt(hg_raw, lb_all, hg_gain, l, batch, seq)
        o_hg_s, s_s = _hgrn_sample(hg_raw, lb_all, hg_gain, state_hgrn, l, rp, bs, ts)
        means = _nsa_means(kv_cmp, batch, seq)
        o_nsa_p = _nsa_prompt(nq, gates, means, kv_sel, kv_win, batch, seq)
        o_nsa_s = _nsa_sample(nq, gates, kv_sel, kv_win, win_buf, cmp_pool, sel_pool, page_table, l, rp, ts)
        o_mla_p = _mla_prompt(q_lat, mqr, c, kr, wuv_t, l, batch, seq)
        o_mla_s = _mla_sample(q_lat, mqr, c, kr, wuv_t, cache_mla_latent, cache_mla_krope, page_table,
                              l, rp, ts)

        o_hg = jnp.concatenate([o_hg_p, o_hg_s], axis=0)
        o_nsa = jnp.concatenate([o_nsa_p, o_nsa_s], axis=0)
        o_mla = jnp.concatenate([o_mla_p, o_mla_s], axis=0)
        y = _merge(x, ng, o_hg, o_nsa, o_mla, w_gate, w_brb, l)
        x = _out_proj(x, y, ng, w_outb, l)
        x = _ffn(x, ng, w_gu, w_dn, l, 1)

        kv4 = lambda a, b, t: a.reshape(b, t, 2, NSA_HD)
        wp = min(WINDOW, seq)
        st_p.append((kv4(kv_cmp[:rp], batch, seq), kv4(kv_sel[:rp], batch, seq),
                     kv4(kv_win[:rp], batch, seq)[:, seq - wp:], c[:rp].reshape(batch, seq, KV_LORA),
                     kr[:rp].reshape(batch, seq, MLA_ROPE), s_p))
        w_all = jnp.concatenate([win_buf[l], kv_win[rp:].reshape(bs, ts, 2 * NSA_HD)], axis=1)
        st_s.append((kv4(kv_cmp[rp:], bs, ts), kv4(kv_sel[rp:], bs, ts), kv4(w_all[:, ts:], bs, wb),
                     c[rp:].reshape(bs, ts, KV_LORA), kr[rp:].reshape(bs, ts, MLA_ROPE), s_s))

    stack = lambda states, i: jnp.stack([s[i] for s in states])
    return ((x[:rp].reshape(batch, seq, d), x[rp:].reshape(bs, ts, d))
            + tuple(stack(st_p, i) for i in range(6)) + tuple(stack(st_s, i) for i in range(6)))
```

```python
import functools

import numpy as np
import jax
import jax.numpy as jnp
from jax import lax
from jax.experimental import pallas as pl
from jax.experimental.pallas import tpu as pltpu

F32 = jnp.float32
BF16 = jnp.bfloat16

PAGE_SIZE = 128
HG_HEADS = 4
HG_DK = 128
HG_DV = 128
HG_WIDTH = HG_HEADS * HG_DV
NSA_HEADS = 8
NSA_HD = 64
NSA_WIDTH = NSA_HEADS * NSA_HD
CMP_BLOCK = 32
SEL_BLOCK = 64
N_SEL = 16
WINDOW = 512
FORCED_IMPORTANCE = 1e4
MLA_HEADS = 8
MLA_NOPE = 64
MLA_ROPE = 32
MLA_VD = 64
KV_LORA = 256
MLA_WIDTH = MLA_HEADS * MLA_VD
ROPE_THETA = 500000.0
NSA_ROT = NSA_HD // 4
N_BRANCH = 3
Q_BLOCK = 128
EPS = 1e-6
NEG_INF = -1e30
ABSENT = -3e38
LOG2E = 1.4426950408889634
MLA_QSCALE = (MLA_NOPE + MLA_ROPE) ** -0.5 * LOG2E
NSA_QSCALE = NSA_HD ** -0.5 * LOG2E

LANES = 128
SUBLANES = 8
VMEM_LIMIT_BYTES = 56 * 1024 * 1024

PJ_NQ = 0
PJ_CMP = PJ_NQ + NSA_WIDTH
PJ_SEL = PJ_CMP + 2 * NSA_HD
PJ_WIN = PJ_SEL + 2 * NSA_HD
PJ_GATE = PJ_WIN + 2 * NSA_HD
PJ_MQN = PJ_GATE + LANES
PJ_MQR = PJ_MQN + MLA_HEADS * MLA_NOPE
PJ_MC = PJ_MQR + MLA_HEADS * MLA_ROPE
PJ_MKR = PJ_MC + KV_LORA
PJ_END = PJ_MKR + LANES


def _params(*sem):
    return pltpu.CompilerParams(dimension_semantics=sem, vmem_limit_bytes=VMEM_LIMIT_BYTES)


def _dot(a, b):
    return jnp.dot(a, b, preferred_element_type=F32)


def _dot_nt(a, b):
    return lax.dot_general(a, b, (((1,), (1,)), ((), ())), preferred_element_type=F32)


def _dot_exact(a, b):
    return jnp.dot(a, b, preferred_element_type=F32, precision=lax.Precision.HIGHEST)


def _rms(x, g):
    return x * lax.rsqrt(jnp.mean(x * x, axis=-1, keepdims=True) + EPS) * g


def _silu(x):
    return x * jax.nn.sigmoid(x)


def _iota(shape, dim):
    return lax.broadcasted_iota(jnp.int32, shape, dim)


def _ffn_body(x_ref, gpre_ref, gpost_ref, wg_ref, wu_ref, wd_ref, o_ref, xn_ref, acc_ref):
    f = pl.program_id(1)

    @pl.when(f == 0)
    def _():
        xn_ref[...] = _rms(x_ref[...], gpre_ref[...]).astype(BF16)
        acc_ref[...] = jnp.zeros_like(acc_ref)

    xn = xn_ref[...]
    a = _dot(xn, wg_ref[...])
    b = _dot(xn, wu_ref[...])
    h = (_silu(a) * b).astype(BF16)
    acc_ref[...] += _dot(h, wd_ref[...])

    @pl.when(f == pl.num_programs(1) - 1)
    def _():
        o_ref[...] = x_ref[...] + 0.5 * _rms(acc_ref[...], gpost_ref[...])


def _ffn(x, norm_g, w_gu, w_dn, layer, which, *, tm=512, tf=512):
    n, d = x.shape
    ff = w_dn.shape[2]
    tm = min(tm, n)
    tf = min(tf, ff)
    nf = ff // tf
    g_pre, g_post = (0, 1) if which == 0 else (4, 5)
    return pl.pallas_call(
        _ffn_body,
        grid=(n // tm, nf),
        in_specs=[
            pl.BlockSpec((tm, d), lambda i, f: (i, 0)),
            pl.BlockSpec((None, None, 1, d), lambda i, f: (layer, g_pre, 0, 0)),
            pl.BlockSpec((None, None, 1, d), lambda i, f: (layer, g_post, 0, 0)),
            pl.BlockSpec((None, None, d, tf), lambda i, f: (layer, which, 0, f)),
            pl.BlockSpec((None, None, d, tf), lambda i, f: (layer, which, 0, f + nf)),
            pl.BlockSpec((None, None, tf, d), lambda i, f: (layer, which, f, 0)),
        ],
        out_specs=pl.BlockSpec((tm, d), lambda i, f: (i, 0)),
        out_shape=jax.ShapeDtypeStruct((n, d), F32),
        scratch_shapes=[pltpu.VMEM((tm, d), BF16), pltpu.VMEM((tm, d), F32)],
        compiler_params=_params("parallel", "arbitrary"),
        name="ffn",
    )(x, norm_g, norm_g, w_gu, w_gu, w_dn)


def _proj_hg_body(x_ref, g_ref, w_ref, o_ref, z_ref):
    @pl.when(pl.program_id(1) == 0)
    def _():
        z_ref[...] = _rms(x_ref[...], g_ref[...]).astype(BF16)

    o_ref[...] = _dot(z_ref[...], w_ref[...])


def _proj_hg(x, norm_g, w_hg, layer, *, tm=512, tn=512):
    n, d = x.shape
    wn = w_hg.shape[2]
    tm = min(tm, n)
    return pl.pallas_call(
        _proj_hg_body,
        grid=(n // tm, wn // tn),
        in_specs=[
            pl.BlockSpec((tm, d), lambda i, j: (i, 0)),
            pl.BlockSpec((None, None, 1, d), lambda i, j: (layer, 2, 0, 0)),
            pl.BlockSpec((None, d, tn), lambda i, j: (layer, 0, j)),
        ],
        out_specs=pl.BlockSpec((tm, tn), lambda i, j: (i, j)),
        out_shape=jax.ShapeDtypeStruct((n, wn), F32),
        scratch_shapes=[pltpu.VMEM((tm, d), BF16)],
        compiler_params=_params("parallel", "arbitrary"),
        name="proj_hg",
    )(x, norm_g, w_hg)


def _rope_lanes(x, tab_ref, base, shift):
    cos, sa, sb = tab_ref[base], tab_ref[base + 1], tab_ref[base + 2]
    return (x * cos + pltpu.roll(x, LANES - shift, 1) * sa + pltpu.roll(x, shift, 1) * sb)


def _proj_nm_body(x_ref, g_ref, w_ref, tab_ref, mg_ref,
                  nq_ref, gate_ref, cmp_ref, sel_ref, win_ref, mqn_ref, mqr_ref, c_ref, kr_ref):
    z = _rms(x_ref[...], g_ref[...]).astype(BF16)
    y = _dot(z, w_ref[...])
    nh = NSA_ROT // 2
    mh = MLA_ROPE // 2
    for c in range(NSA_WIDTH // LANES):
        nq_ref[:, c * LANES:(c + 1) * LANES] = _rope_lanes(
            y[:, PJ_NQ + c * LANES:PJ_NQ + (c + 1) * LANES], tab_ref, 0, nh)
    cmp_ref[...] = _rope_lanes(y[:, PJ_CMP:PJ_CMP + LANES], tab_ref, 3, nh)
    sel_ref[...] = _rope_lanes(y[:, PJ_SEL:PJ_SEL + LANES], tab_ref, 3, nh)
    win_ref[...] = _rope_lanes(y[:, PJ_WIN:PJ_WIN + LANES], tab_ref, 3, nh)
    gate_ref[...] = jax.nn.sigmoid(y[:, PJ_GATE:PJ_GATE + LANES])
    mqn_ref[...] = y[:, PJ_MQN:PJ_MQR]
    for c in range(MLA_HEADS * MLA_ROPE // LANES):
        mqr_ref[:, c * LANES:(c + 1) * LANES] = _rope_lanes(
            y[:, PJ_MQR + c * LANES:PJ_MQR + (c + 1) * LANES], tab_ref, 6, mh)
    c_ref[...] = _rms(y[:, PJ_MC:PJ_MKR], mg_ref[...])
    kr_ref[...] = _rope_lanes(y[:, PJ_MKR:PJ_END], tab_ref, 6, mh)[:, :MLA_ROPE]


def _proj_nm(x, norm_g, w_nm, tabs, mla_norm, layer, *, tm=256):
    n, d = x.shape
    tm = min(tm, n)
    row = lambda i: (i, 0)
    widths = (NSA_WIDTH, LANES, LANES, LANES, LANES, MLA_HEADS * MLA_NOPE, MLA_HEADS * MLA_ROPE,
              KV_LORA, MLA_ROPE)
    return pl.pallas_call(
        _proj_nm_body,
        grid=(n // tm,),
        in_specs=[
            pl.BlockSpec((tm, d), row),
            pl.BlockSpec((None, None, 1, d), lambda i: (layer, 2, 0, 0)),
            pl.BlockSpec((None, d, PJ_END), lambda i: (layer, 0, 0)),
            pl.BlockSpec((9, tm, LANES), lambda i: (0, i, 0)),
            pl.BlockSpec((None, 1, KV_LORA), lambda i: (layer, 0, 0)),
        ],
        out_specs=[pl.BlockSpec((tm, w), row) for w in widths],
        out_shape=[jax.ShapeDtypeStruct((n, w), F32) for w in widths],
        compiler_params=_params("parallel"),
        name="proj_nsa_mla",
    )(x, norm_g, w_nm, tabs, mla_norm)


def _absorb_body(mqn_ref, wuk_ref, o_ref, *, scale):
    for h in range(MLA_HEADS):
        q = mqn_ref[:, h * MLA_NOPE:(h + 1) * MLA_NOPE].astype(BF16)
        o_ref[:, h * KV_LORA:(h + 1) * KV_LORA] = _dot(q, wuk_ref[h]) * scale


def _mla_absorb(mqn, wuk_t, layer, *, tm=512):
    n = mqn.shape[0]
    tm = min(tm, n)
    return pl.pallas_call(
        functools.partial(_absorb_body, scale=MLA_QSCALE),
        grid=(n // tm,),
        in_specs=[
            pl.BlockSpec((tm, MLA_HEADS * MLA_NOPE), lambda i: (i, 0)),
            pl.BlockSpec((None, MLA_HEADS, MLA_NOPE, KV_LORA), lambda i: (layer, 0, 0, 0)),
        ],
        out_specs=pl.BlockSpec((tm, MLA_HEADS * KV_LORA), lambda i: (i, 0)),
        out_shape=jax.ShapeDtypeStruct((n, MLA_HEADS * KV_LORA), F32),
        compiler_params=_params("parallel"),
        name="mla_absorb",
    )(mqn, wuk_t)


def _merge_body(x_ref, g_ref, o0_ref, o1_ref, o2_ref, wg0_ref, wg1_ref, wg2_ref,
                wb0_ref, wb1_ref, wb2_ref, y_ref, z_ref):
    @pl.when(pl.program_id(1) == 0)
    def _():
        z_ref[...] = _rms(x_ref[...], g_ref[...]).astype(BF16)

    z = z_ref[...]
    y = jnp.zeros(y_ref.shape, F32)
    for o_ref, wg_ref, wb_ref in ((o0_ref, wg0_ref, wb0_ref), (o1_ref, wg1_ref, wb1_ref),
                                  (o2_ref, wg2_ref, wb2_ref)):
        gate = jax.nn.sigmoid(_dot(z, wg_ref[...]))
        y = y + gate * _dot(o_ref[...].astype(BF16), wb_ref[...])
    y_ref[...] = y.astype(BF16)


def _merge(x, norm_g, o_hg, o_nsa, o_mla, w_gate, w_br, layer, *, tm=512, tn=512):
    n, d = x.shape
    tm = min(tm, n)
    nt = d // tn
    wb = w_br.shape[1] // N_BRANCH
    row = lambda i, j: (i, 0)
    return pl.pallas_call(
        _merge_body,
        grid=(n // tm, nt),
        in_specs=[
            pl.BlockSpec((tm, d), row),
            pl.BlockSpec((None, None, 1, d), lambda i, j: (layer, 2, 0, 0)),
            pl.BlockSpec((tm, wb), row), pl.BlockSpec((tm, wb), row), pl.BlockSpec((tm, wb), row),
            pl.BlockSpec((None, d, tn), lambda i, j: (layer, 0, j)),
            pl.BlockSpec((None, d, tn), lambda i, j: (layer, 0, nt + j)),
            pl.BlockSpec((None, d, tn), lambda i, j: (layer, 0, 2 * nt + j)),
            pl.BlockSpec((None, wb, tn), lambda i, j: (layer, 0, j)),
            pl.BlockSpec((None, wb, tn), lambda i, j: (layer, 1, j)),
            pl.BlockSpec((None, wb, tn), lambda i, j: (layer, 2, j)),
        ],
        out_specs=pl.BlockSpec((tm, tn), lambda i, j: (i, j)),
        out_shape=jax.ShapeDtypeStruct((n, d), BF16),
        scratch_shapes=[pltpu.VMEM((tm, d), BF16)],
        compiler_params=_params("parallel", "arbitrary"),
        name="merge",
    )(x, norm_g, o_hg, o_nsa, o_mla, w_gate, w_gate, w_gate, w_br, w_br, w_br)


def _out_proj_body(x_ref, y_ref, g_ref, w_ref, o_ref):
    o_ref[...] = x_ref[...] + _rms(_dot(y_ref[...], w_ref[...]), g_ref[...])


def _out_proj(x, y, norm_g, w_out, layer, *, tm=512):
    n, d = x.shape
    tm = min(tm, n)
    row = lambda i: (i, 0)
    return pl.pallas_call(
        _out_proj_body,
        grid=(n // tm,),
        in_specs=[
            pl.BlockSpec((tm, d), row),
            pl.BlockSpec((tm, d), row),
            pl.BlockSpec((None, None, 1, d), lambda i: (layer, 3, 0, 0)),
            pl.BlockSpec((None, d, d), lambda i: (layer, 0, 0)),
        ],
        out_specs=pl.BlockSpec((tm, d), row),
        out_shape=jax.ShapeDtypeStruct((n, d), F32),
        compiler_params=_params("parallel"),
        name="out_proj",
    )(x, y, norm_g, w_out)


HG_ROWS = 128
HG_DIAG = SUBLANES


def _lb_body(p_ref, o_ref):
    p = p_ref[...]
    e = jnp.exp(p - jnp.max(p, axis=0, keepdims=True))
    sm = e / jnp.sum(e, axis=0, keepdims=True)
    run = jnp.zeros_like(sm[0:1])
    for l in range(p.shape[0]):
        run = run + sm[l:l + 1]
        o_ref[l:l + 1, :] = run - sm[0:1]


def _hgrn_lower_bounds(hg_lb):
    return pl.pallas_call(
        _lb_body,
        out_shape=jax.ShapeDtypeStruct(hg_lb.shape, F32),
        name="hgrn_lower_bounds",
    )(hg_lb)


def _hg_inputs(x, lb, h):
    w = HG_DK
    hq = x[:, h * w:(h + 1) * w]
    hf = x[:, HG_WIDTH + h * w:HG_WIDTH + (h + 1) * w]
    v = x[:, 2 * HG_WIDTH + h * w:2 * HG_WIDTH + (h + 1) * w]
    g = x[:, 3 * HG_WIDTH + h * w:3 * HG_WIDTH + (h + 1) * w]
    lbh = lb[:, h * w:(h + 1) * w]
    q = _silu(hq)
    log_f = jnp.log(lbh + (1.0 - lbh) * jax.nn.sigmoid(hf))
    k = (1.0 - lbh) * jax.nn.sigmoid(-hf)
    return q, log_f, k, v, g


def _group_last(b, size):
    n = b.shape[0] // size
    b3 = b.reshape(n, size, b.shape[1])
    return jnp.broadcast_to(b3[:, size - 1:size, :], b3.shape).reshape(b.shape)


def _group_mid(b, half):
    n = b.shape[0] // (2 * half)
    b3 = b.reshape(n, 2 * half, b.shape[1])
    return jnp.broadcast_to(b3[:, half - 1:half, :], b3.shape).reshape(b.shape)


def _diag_blocks(q, k, b, rows, cols):
    a = jnp.zeros((q.shape[0], q.shape[0]), F32)
    rin = rows % HG_DIAG
    for d in range(HG_DIAG):
        if d == 0:
            kd, bd = k, b
        else:
            kd, bd = pltpu.roll(k, d, 0), pltpu.roll(b, d, 0)
        w = jnp.sum(q * kd * jnp.exp(jnp.minimum(b - bd, 0.0)), axis=-1, keepdims=True)
        a = a + jnp.where((cols == rows - d) & (rin >= d), w, 0.0)
    return a


def _cross_blocks(q, k, b, rows, cols, top):
    a = jnp.zeros((q.shape[0], q.shape[0]), F32)
    half = top // 2
    while half >= HG_DIAG:
        mid = _group_mid(b, half)
        right = (rows % (2 * half)) >= half
        e = jnp.exp(-jnp.abs(b - mid))
        lf = jnp.where(right, q * e, 0.0).astype(BF16)
        rf = jnp.where(right, 0.0, k * e).astype(BF16)
        same = (rows // (2 * half)) == (cols // (2 * half))
        a = a + jnp.where(same, _dot_nt(lf, rf), 0.0)
        half //= 2
    return a


def _col_of_row(e_row, rows, cols):
    return jnp.sum(jnp.where(rows == cols, e_row, 0.0), axis=-1, keepdims=True)


def _hg_readout(o, g, gain):
    return _rms(o, gain) * _silu(g)


def _hgrn_prompt_body(x_ref, lb_ref, gain_ref, o_ref, s_out_ref, s_ref):
    c = pl.program_id(1)

    @pl.when(c == 0)
    def _():
        s_ref[...] = jnp.zeros_like(s_ref)

    n = HG_ROWS
    rows, cols = _iota((n, n), 0), _iota((n, n), 1)
    tri = (cols <= rows).astype(F32)
    x = x_ref[...]
    lb = lb_ref[...]
    for h in range(HG_HEADS):
        q, log_f, k, v, g = _hg_inputs(x, lb, h)
        b = _dot_exact(tri, log_f)
        a = _diag_blocks(q, k, b, rows, cols) + _cross_blocks(q, k, b, rows, cols, n)
        s = s_ref[h]
        vb = v.astype(BF16)
        o = _dot((q * jnp.exp(b)).astype(BF16), s.astype(BF16)) + _dot(a.astype(BF16), vb)
        b_last = b[n - 1:n, :]
        kp = k * jnp.exp(b_last - b)
        s_new = _col_of_row(jnp.exp(b_last), rows, cols) * s + _dot(kp.T.astype(BF16), vb)
        s_ref[h] = s_new
        o_ref[:, h * HG_DV:(h + 1) * HG_DV] = _hg_readout(o, g, gain_ref[...])

    @pl.when(c == pl.num_programs(1) - 1)
    def _():
        s_out_ref[...] = s_ref[...]


def _hgrn_prompt(hg_raw, lb, gain, layer, batch, seq):
    nc = seq // HG_ROWS
    return pl.pallas_call(
        _hgrn_prompt_body,
        grid=(batch, nc),
        in_specs=[
            pl.BlockSpec((HG_ROWS, 4 * HG_WIDTH), lambda b, c: (b * nc + c, 0)),
            pl.BlockSpec((None, 1, HG_WIDTH), lambda b, c: (layer, 0, 0)),
            pl.BlockSpec((None, 1, HG_DV), lambda b, c: (layer, 0, 0)),
        ],
        out_specs=[
            pl.BlockSpec((HG_ROWS, HG_WIDTH), lambda b, c: (b * nc + c, 0)),
            pl.BlockSpec((None, HG_HEADS, HG_DK, HG_DV), lambda b, c: (b, 0, 0, 0)),
        ],
        out_shape=[jax.ShapeDtypeStruct((batch * seq, HG_WIDTH), F32),
                   jax.ShapeDtypeStruct((batch, HG_HEADS, HG_DK, HG_DV), F32)],
        scratch_shapes=[pltpu.VMEM((HG_HEADS, HG_DK, HG_DV), F32)],
        compiler_params=_params("parallel", "arbitrary"),
        name="hgrn_prompt",
    )(hg_raw, lb, gain)


def _hgrn_sample_body(x_ref, lb_ref, gain_ref, s0_ref, o_ref, s_out_ref, *, t):
    n = HG_ROWS
    nseq = n // t
    rows, cols = _iota((n, n), 0), _iota((n, n), 1)
    tri = ((cols <= rows) & (rows // t == cols // t)).astype(F32)
    x = x_ref[...]
    lb = lb_ref[...]
    for h in range(HG_HEADS):
        q, log_f, k, v, g = _hg_inputs(x, lb, h)
        b = _dot_exact(tri, log_f)
        a = _diag_blocks(q, k, b, rows, cols)
        vb = v.astype(BF16)
        o = _dot(a.astype(BF16), vb)
        eb = jnp.exp(b)
        qe = q * eb
        b_last = _group_last(b, t)
        kpt = (k * jnp.exp(b_last - b)).T
        for i in range(nseq):
            mine = (rows // t) == i
            s0 = s0_ref[i, h]
            o = o + _dot(jnp.where(mine, qe, 0.0).astype(BF16), s0.astype(BF16))
            e_col = _col_of_row(eb[i * t + t - 1:i * t + t, :], rows, cols)
            upd = _dot(jnp.where((cols // t) == i, kpt, 0.0).astype(BF16), vb)
            s_out_ref[i, h] = e_col * s0 + upd
        o_ref[:, h * HG_DV:(h + 1) * HG_DV] = _hg_readout(o, g, gain_ref[...])


def _hgrn_sample(hg_raw, lb, gain, state, layer, row0, batch, t):
    assert t == HG_DIAG and HG_ROWS % t == 0
    nseq = HG_ROWS // t
    blk0 = row0 // HG_ROWS
    return pl.pallas_call(
        functools.partial(_hgrn_sample_body, t=t),
        grid=(batch // nseq,),
        in_specs=[
            pl.BlockSpec((HG_ROWS, 4 * HG_WIDTH), lambda i: (blk0 + i, 0)),
            pl.BlockSpec((None, 1, HG_WIDTH), lambda i: (layer, 0, 0)),
            pl.BlockSpec((None, 1, HG_DV), lambda i: (layer, 0, 0)),
            pl.BlockSpec((None, nseq, HG_HEADS, HG_DK, HG_DV), lambda i: (layer, i, 0, 0, 0)),
        ],
        out_specs=[
            pl.BlockSpec((HG_ROWS, HG_WIDTH), lambda i: (i, 0)),
            pl.BlockSpec((nseq, HG_HEADS, HG_DK, HG_DV), lambda i: (i, 0, 0, 0)),
        ],
        out_shape=[jax.ShapeDtypeStruct((batch * t, HG_WIDTH), F32),
                   jax.ShapeDtypeStruct((batch, HG_HEADS, HG_DK, HG_DV), F32)],
        compiler_params=_params("parallel"),
        name="hgrn_sample",
    )(hg_raw, lb, gain, state)


def _stack_heads(x, heads, width):
    return jnp.concatenate([x[:, h * width:(h + 1) * width] for h in range(heads)], axis=0)


def _mla_up(o_lat, wuv_ref, o_ref, t):
    for h in range(MLA_HEADS):
        ol = o_lat[h * t:(h + 1) * t, :].astype(BF16)
        o_ref[:, h * MLA_VD:(h + 1) * MLA_VD] = _dot(ol, wuv_ref[h])


def _mla_prompt_body(ql_in_ref, qr_in_ref, c_ref, kr_ref, wuv_ref, o_ref,
                     ql_ref, qr_ref, m_ref, l_ref, acc_ref, *, tq, tk, scale):
    i, j = pl.program_id(1), pl.program_id(2)
    q0, k0 = i * tq, j * tk
    last_j = (q0 + tq - 1) // tk

    @pl.when(j == 0)
    def _():
        ql_ref[...] = _stack_heads(ql_in_ref[...], MLA_HEADS, KV_LORA).astype(BF16)
        qr_ref[...] = (_stack_heads(qr_in_ref[...], MLA_HEADS, MLA_ROPE) * scale).astype(BF16)
        m_ref[...] = jnp.full_like(m_ref, NEG_INF)
        l_ref[...] = jnp.zeros_like(l_ref)
        acc_ref[...] = jnp.zeros_like(acc_ref)

    def step(causal):
        cb = c_ref[...].astype(BF16)
        kb = kr_ref[...].astype(BF16)
        s = _dot_nt(ql_ref[...], cb) + _dot_nt(qr_ref[...], kb)
        if causal:
            qpos = q0 + (_iota(s.shape, 0) & (tq - 1))
            s = jnp.where(k0 + _iota(s.shape, 1) <= qpos, s, NEG_INF)
        m_prev = m_ref[...]
        m_new = jnp.maximum(m_prev, jnp.max(s, axis=-1, keepdims=True))
        alpha = jnp.exp2(m_prev - m_new)
        p = jnp.exp2(s - m_new)
        l_ref[...] = alpha * l_ref[...] + jnp.sum(p, axis=-1, keepdims=True)
        acc_ref[...] = alpha * acc_ref[...] + _dot(p.astype(BF16), cb)
        m_ref[...] = m_new

    @pl.when(j < last_j)
    def _():
        step(False)

    @pl.when(j == last_j)
    def _():
        step(True)
        _mla_up(acc_ref[...] / l_ref[...], wuv_ref, o_ref, tq)


def _mla_prompt(q_lat, mqr, c, kr, wuv, layer, batch, seq, *, tq=128, tk=1024):
    tk = min(tk, seq)
    assert tk % tq == 0
    nq, nk = seq // tq, seq // tk
    scale = MLA_QSCALE
    hq = MLA_HEADS * tq

    def kv_map(b, i, j):
        return (b * nk + jnp.minimum(j, (i * tq + tq - 1) // tk), 0)

    return pl.pallas_call(
        functools.partial(_mla_prompt_body, tq=tq, tk=tk, scale=scale),
        grid=(batch, nq, nk),
        in_specs=[
            pl.BlockSpec((tq, MLA_HEADS * KV_LORA), lambda b, i, j: (b * nq + i, 0)),
            pl.BlockSpec((tq, MLA_HEADS * MLA_ROPE), lambda b, i, j: (b * nq + i, 0)),
            pl.BlockSpec((tk, KV_LORA), kv_map),
            pl.BlockSpec((tk, MLA_ROPE), kv_map),
            pl.BlockSpec((None, MLA_HEADS, KV_LORA, MLA_VD), lambda b, i, j: (layer, 0, 0, 0)),
        ],
        out_specs=pl.BlockSpec((tq, MLA_WIDTH), lambda b, i, j: (b * nq + i, 0)),
        out_shape=jax.ShapeDtypeStruct((batch * seq, MLA_WIDTH), F32),
        scratch_shapes=[pltpu.VMEM((hq, KV_LORA), BF16), pltpu.VMEM((hq, MLA_ROPE), BF16),
                        pltpu.VMEM((hq, 1), F32), pltpu.VMEM((hq, 1), F32),
                        pltpu.VMEM((hq, KV_LORA), F32)],
        compiler_params=_params("parallel", "parallel", "arbitrary"),
        name="mla_prompt",
    )(q_lat, mqr, c, kr, wuv)


NEW_PAD = 16


def _pad_rows(x, rows):
    return jnp.concatenate([x, jnp.zeros((rows - x.shape[0], x.shape[1]), x.dtype)], axis=0)


def _page_copies(pt_ref, pool_ref, buf_ref, sem_ref, layer, seq, slot, rows_last):
    def copy(p):
        rows = pl.ds(p * PAGE_SIZE, PAGE_SIZE)
        dst = buf_ref.at[(slot,) + (slice(None),) * (len(buf_ref.shape) - 2) + (rows,)] if rows_last \
            else buf_ref.at[slot, rows]
        return pltpu.make_async_copy(pool_ref.at[layer, pt_ref[seq, p]], dst, sem_ref.at[slot])
    return copy


def _fetch_pages(pt_ref, pools, sems, layer, n_pages, seq, slot, start):
    for k, (pool, buf, rows_last) in enumerate(pools):
        copy = _page_copies(pt_ref, pool, buf, sems.at[k], layer, seq, slot, rows_last)
        (_start_pages if start else _wait_pages)(copy, n_pages)


def _double_buffered_pages(pt_ref, pools, sems, layer, n_pages):
    b = pl.program_id(0)
    slot = b % 2

    @pl.when(b == 0)
    def _():
        _fetch_pages(pt_ref, pools, sems, layer, n_pages, 0, 0, True)

    @pl.when(b + 1 < pl.num_programs(0))
    def _():
        _fetch_pages(pt_ref, pools, sems, layer, n_pages, b + 1, 1 - slot, True)

    _fetch_pages(pt_ref, pools, sems, layer, n_pages, b, slot, False)
    return slot


def _softmax_part(s, pv, ok=None):
    if ok is not None:
        s = jnp.where(ok, s, NEG_INF)
    m = jnp.max(s, axis=-1, keepdims=True)
    p = jnp.exp2(s - m)
    if ok is not None:
        p = jnp.where(ok, p, 0.0)
    return m, jnp.sum(p, axis=-1, keepdims=True), pv(p.astype(BF16))


def _merge_parts(parts):
    m = parts[0][0]
    for part in parts[1:]:
        m = jnp.maximum(m, part[0])
    l, acc = 0.0, 0.0
    for mi, li, ai in parts:
        w = jnp.exp2(mi - m)
        l = l + w * li
        acc = acc + w * ai
    return acc / l


def _start_pages(copy, n_pages):
    def body(p, carry):
        copy(p).start()
        return carry
    lax.fori_loop(0, n_pages, body, 0)


def _wait_pages(copy, n_pages):
    def body(p, carry):
        copy(p).wait()
        return carry
    lax.fori_loop(0, n_pages, body, 0)


def _mla_sample_body(pt_ref, ql_in_ref, qr_in_ref, c_new_ref, kr_new_ref, wuv_ref, lat_hbm, krt_hbm,
                     o_ref, lat_buf, krt_buf, sems, *, layer, t, n_pages, chunk):
    pools = ((lat_hbm, lat_buf, False), (krt_hbm, krt_buf, True))
    slot = _double_buffered_pages(pt_ref, pools, sems, layer, n_pages)
    past = n_pages * PAGE_SIZE

    ql = _stack_heads(ql_in_ref[...], MLA_HEADS, KV_LORA).astype(BF16)
    qr = (_stack_heads(qr_in_ref[...], MLA_HEADS, MLA_ROPE) * MLA_QSCALE).astype(BF16)
    c_new = _pad_rows(c_new_ref[...], NEW_PAD).astype(BF16)
    kr_new = _pad_rows(kr_new_ref[...], NEW_PAD).astype(BF16)
    s_new = _dot_nt(ql, c_new) + _dot_nt(qr, kr_new)
    tok = _iota(s_new.shape, 0) & (t - 1)
    parts = [_softmax_part(s_new, lambda p: _dot(p, c_new), _iota(s_new.shape, 1) <= tok)]
    for ch in range(past // chunk):
        cb = lat_buf[slot, ch * chunk:(ch + 1) * chunk, :].astype(BF16)
        kb = krt_buf[slot, :, ch * chunk:(ch + 1) * chunk].astype(BF16)
        parts.append(_softmax_part(_dot_nt(ql, cb) + _dot(qr, kb), lambda p, cb=cb: _dot(p, cb)))
    _mla_up(_merge_parts(parts), wuv_ref, o_ref, t)


def _mla_sample(q_lat, mqr, c, kr, wuv, lat_pool, krt_pool, page_table, layer, row0, t, *, chunk=1024):
    batch, n_pages = page_table.shape
    assert t & (t - 1) == 0 and t <= NEW_PAD
    past = n_pages * PAGE_SIZE
    chunk = min(chunk, past)
    blk0 = row0 // t
    tok = lambda i, pt: (blk0 + i, 0)
    grid_spec = pltpu.PrefetchScalarGridSpec(
        num_scalar_prefetch=1,
        grid=(batch,),
        in_specs=[
            pl.BlockSpec((t, MLA_HEADS * KV_LORA), tok),
            pl.BlockSpec((t, MLA_HEADS * MLA_ROPE), tok),
            pl.BlockSpec((t, KV_LORA), tok),
            pl.BlockSpec((t, MLA_ROPE), tok),
            pl.BlockSpec((None, MLA_HEADS, KV_LORA, MLA_VD), lambda i, pt: (layer, 0, 0, 0)),
            pl.BlockSpec(memory_space=pl.ANY),
            pl.BlockSpec(memory_space=pl.ANY),
        ],
        out_specs=pl.BlockSpec((t, MLA_WIDTH), lambda i, pt: (i, 0)),
        scratch_shapes=[pltpu.VMEM((2, past, KV_LORA), F32), pltpu.VMEM((2, MLA_ROPE, past), F32),
                        pltpu.SemaphoreType.DMA((2, 2))],
    )
    return pl.pallas_call(
        functools.partial(_mla_sample_body, layer=layer, t=t, n_pages=n_pages, chunk=chunk),
        grid_spec=grid_spec,
        out_shape=jax.ShapeDtypeStruct((batch * t, MLA_WIDTH), F32),
        compiler_params=_params("arbitrary"),
        name="mla_sample",
    )(page_table, q_lat, mqr, c, kr, wuv, lat_pool, krt_pool)


def _block_means(x):
    pairs = x.reshape(x.shape[0] // SEL_BLOCK, SEL_BLOCK, x.shape[1])
    even = jnp.sum(pairs[:, :CMP_BLOCK, :], axis=1)
    odd = jnp.sum(pairs[:, CMP_BLOCK:, :], axis=1)
    return jnp.concatenate([even, odd], axis=0) / CMP_BLOCK


def _means_body(x_ref, o_ref):
    o_ref[...] = _block_means(x_ref[...])


def _nsa_means(kv_cmp, batch, seq):
    return pl.pallas_call(
        _means_body,
        grid=(batch,),
        in_specs=[pl.BlockSpec((seq, LANES), lambda b: (b, 0))],
        out_specs=pl.BlockSpec((None, seq // CMP_BLOCK, LANES), lambda b: (b, 0, 0)),
        out_shape=jax.ShapeDtypeStruct((batch, seq // CMP_BLOCK, LANES), F32),
        compiler_params=_params("parallel"),
        name="nsa_means",
    )(kv_cmp)


def _softmax_rows(s, mask):
    if mask is not None:
        s = jnp.where(mask, s, NEG_INF)
    m = jnp.max(s, axis=-1, keepdims=True)
    e = jnp.exp2(s - m)
    if mask is not None:
        e = jnp.where(mask, e, 0.0)
    l = jnp.sum(e, axis=-1, keepdims=True)
    return e / jnp.where(l > 0.0, l, 1.0)


def _select_blocks(imp, qpos, ns):
    blk = _iota(imp.shape, 1)
    cur = qpos // SEL_BLOCK
    forced = (blk == 0) | (blk == cur) | (blk == cur - 1)
    key = jnp.where(forced, imp + FORCED_IMPORTANCE, imp)
    key = jnp.where(blk * SEL_BLOCK <= qpos, key, NEG_INF)
    key = jnp.where(blk < ns, key, ABSENT)
    cnt = jnp.zeros(imp.shape, F32)
    for i in range(ns):
        col = key[:, i:i + 1]
        beats = (col > key) | ((col == key) & (blk > i))
        cnt = cnt + jnp.where(beats, 1.0, 0.0)
    return cnt < float(min(N_SEL, ns))


def _pad_lanes(x, lanes):
    if x.shape[1] == lanes:
        return x
    return jnp.concatenate([x, jnp.zeros((x.shape[0], lanes - x.shape[1]), x.dtype)], axis=1)


def _nsa_combine(gate, o_c, o_s, o_w, o_ref, t):
    for h in range(NSA_HEADS):
        r = slice(h * t, (h + 1) * t)
        o_ref[:, h * NSA_HD:(h + 1) * NSA_HD] = (gate[:, 3 * h:3 * h + 1] * o_c[r]
                                                 + gate[:, 3 * h + 1:3 * h + 2] * o_s[r]
                                                 + gate[:, 3 * h + 2:3 * h + 3] * o_w[r])


def _nsa_prompt_body(nq_ref, gate_ref, mean_ref, sel_ref, win_ref, exp_ref, o_ref, mp_ref, lp_ref, ap_ref,
                     *, tq, tk, seq, wlen):
    i = pl.program_id(1)
    q0 = i * tq
    rows = NSA_HEADS * tq
    q = (_stack_heads(nq_ref[...], NSA_HEADS, NSA_HD) * NSA_QSCALE).astype(BF16)
    tok_pos = q0 + (_iota((rows, 1), 0) & (tq - 1))
    qpos = q0 + _iota((tq, 1), 0)

    nc = seq // CMP_BLOCK
    ns = seq // SEL_BLOCK
    means = mean_ref[...]
    kc, vc = means[:, :NSA_HD].astype(BF16), means[:, NSA_HD:].astype(BF16)
    s_c = _dot_nt(q, kc)
    col = _iota(s_c.shape, 1)
    cid = jnp.where(col < ns, 2 * col, 2 * (col - ns) + 1)
    p_c = _softmax_rows(s_c, (cid + 1) * CMP_BLOCK - 1 <= tok_pos)
    o_c = _dot(p_c.astype(BF16), vc)
    imp = jnp.sum(p_c.reshape(NSA_HEADS, tq, nc), axis=0)
    lanes = -(-ns // LANES) * LANES
    imp = _pad_lanes(imp[:, :ns] + imp[:, ns:], lanes)
    chosen = _select_blocks(imp, qpos, ns) & (_iota((tq, lanes), 1) * SEL_BLOCK <= qpos)
    bias = jnp.where(chosen, 0.0, NEG_INF)

    nb = tk // SEL_BLOCK
    last_j = q0 // tk
    for j in range(seq // tk):
        def part(causal, j=j):
            kv = sel_ref[j * tk:(j + 1) * tk, :]
            ks, vs = kv[:, :NSA_HD].astype(BF16), kv[:, NSA_HD:].astype(BF16)
            tile_bias = _dot(bias[:, j * nb:(j + 1) * nb].astype(BF16), exp_ref[...])
            if causal:
                tile_bias = jnp.where(j * tk + _iota((tq, tk), 1) <= qpos, tile_bias, NEG_INF)
            s = (_dot_nt(q, ks).reshape(NSA_HEADS, tq, tk) + tile_bias[None]).reshape(rows, tk)
            mp_ref[j], lp_ref[j], ap_ref[j] = _softmax_part(s, lambda p: _dot(p, vs))

        @pl.when(j < last_j)
        def _():
            part(False)

        @pl.when(j == last_j)
        def _():
            part(True)

        @pl.when(j > last_j)
        def _():
            mp_ref[j] = jnp.full((rows, 1), NEG_INF, F32)
            lp_ref[j] = jnp.zeros((rows, 1), F32)
            ap_ref[j] = jnp.zeros((rows, NSA_HD), F32)

    o_s = _merge_parts([(mp_ref[j], lp_ref[j], ap_ref[j]) for j in range(seq // tk)])

    w0 = pl.multiple_of(jnp.clip(q0 + tq - wlen, 0, seq - wlen), SUBLANES)
    wkv = win_ref[pl.ds(w0, wlen), :]
    kw, vw = wkv[:, :NSA_HD].astype(BF16), wkv[:, NSA_HD:].astype(BF16)
    s_w = _dot_nt(q, kw)
    wpos = w0 + _iota(s_w.shape, 1)
    p_w = _softmax_rows(s_w, (wpos <= tok_pos) & (wpos > tok_pos - WINDOW))
    o_w = _dot(p_w.astype(BF16), vw)
    _nsa_combine(gate_ref[...], o_c, o_s, o_w, o_ref, tq)


def _expand_matrix(lanes, keys, tile):
    e = (np.arange(keys)[None, :] // SEL_BLOCK) == np.arange(lanes)[:, None]
    e = e.reshape(lanes, keys // tile, tile).transpose(1, 0, 2)
    return jnp.asarray(e, BF16)


def _nsa_prompt(nq, gates, means, kv_sel, kv_win, batch, seq, *, tq=Q_BLOCK, tk=1024):
    tk = min(tk, seq)
    assert tk % tq == 0 and seq % tk == 0
    nqb = seq // tq
    nk = seq // tk
    rows = NSA_HEADS * tq
    wlen = min(WINDOW + tq, seq)
    expand = _expand_matrix(tk // SEL_BLOCK, tk, tk)[0]
    tok = lambda b, i: (b * nqb + i, 0)
    return pl.pallas_call(
        functools.partial(_nsa_prompt_body, tq=tq, tk=tk, seq=seq, wlen=wlen),
        grid=(batch, nqb),
        in_specs=[
            pl.BlockSpec((tq, NSA_WIDTH), tok),
            pl.BlockSpec((tq, LANES), tok),
            pl.BlockSpec((None, seq // CMP_BLOCK, LANES), lambda b, i: (b, 0, 0)),
            pl.BlockSpec((seq, LANES), lambda b, i: (b, 0)),
            pl.BlockSpec((seq, LANES), lambda b, i: (b, 0)),
            pl.BlockSpec(expand.shape, lambda b, i: (0, 0)),
        ],
        out_specs=pl.BlockSpec((tq, NSA_WIDTH), tok),
        out_shape=jax.ShapeDtypeStruct((batch * seq, NSA_WIDTH), F32),
        scratch_shapes=[pltpu.VMEM((nk, rows, 1), F32), pltpu.VMEM((nk, rows, 1), F32),
                        pltpu.VMEM((nk, rows, NSA_HD), F32)],
        compiler_params=_params("parallel", "parallel"),
        name="nsa_prompt",
    )(nq, gates, means, kv_sel, kv_win, expand)


def _nsa_sample_body(pt_ref, nq_ref, gate_ref, sel_new_ref, win_new_ref, wbuf_ref, pool_ref, exp_ref,
                     cmp_hbm, sel_hbm, o_ref, cmp_buf, sel_buf, sems, *, layer, t, n_pages, chunk):
    pools = ((cmp_hbm, cmp_buf, True), (sel_hbm, sel_buf, True))
    slot = _double_buffered_pages(pt_ref, pools, sems, layer, n_pages)
    past = n_pages * PAGE_SIZE
    rows = NSA_HEADS * t
    q = (_stack_heads(nq_ref[...], NSA_HEADS, NSA_HD) * NSA_QSCALE).astype(BF16)
    tok = _iota((rows, 1), 0) & (t - 1)

    ncp = past // CMP_BLOCK
    nsp = past // SEL_BLOCK
    ns = nsp + 1
    means = jnp.zeros((2 * NSA_HD, ncp), F32)
    for ch in range(past // chunk):
        x = cmp_buf[slot, :, :, ch * chunk:(ch + 1) * chunk].reshape(2 * NSA_HD, chunk)
        hi = x.astype(BF16)
        lo = (x - hi.astype(F32)).astype(BF16)
        pooled = _dot(jnp.concatenate([hi, lo], axis=0), pool_ref[ch])
        means = means + pooled[:2 * NSA_HD] + pooled[2 * NSA_HD:]
    kct, vct = means[:NSA_HD].astype(BF16), means[NSA_HD:].astype(BF16)
    s_c = _dot(q, kct)
    e_c = jnp.exp2(s_c - jnp.max(s_c, axis=-1, keepdims=True))
    p_c = e_c / jnp.sum(e_c, axis=-1, keepdims=True)
    o_c = _dot_nt(p_c.astype(BF16), vct)
    imp = jnp.sum(p_c.reshape(NSA_HEADS, t, ncp), axis=0)
    lanes = -(-ns // LANES) * LANES
    imp = _pad_lanes(imp[:, :nsp] + imp[:, nsp:], lanes)
    chosen = _select_blocks(imp, past + _iota((t, 1), 0), ns)
    bias = jnp.concatenate([jnp.where(chosen, 0.0, NEG_INF)] * NSA_HEADS, axis=0)

    sel_new = _pad_rows(sel_new_ref[...], NEW_PAD)
    k_new, v_new = sel_new[:, :NSA_HD].astype(BF16), sel_new[:, NSA_HD:].astype(BF16)
    s_new = _dot_nt(q, k_new)
    ok_new = (_iota(s_new.shape, 1) <= tok) & (bias[:, nsp:nsp + 1] > -1.0)
    parts = [_softmax_part(s_new, lambda p: _dot(p, v_new), ok_new)]
    nb = chunk // SEL_BLOCK
    for ch in range(past // chunk):
        kt = sel_buf[slot, 0, :, ch * chunk:(ch + 1) * chunk].astype(BF16)
        vt = sel_buf[slot, 1, :, ch * chunk:(ch + 1) * chunk].astype(BF16)
        s = _dot(q, kt) + _dot(bias[:, ch * nb:(ch + 1) * nb].astype(BF16), exp_ref[...])
        parts.append(_softmax_part(s, lambda p, vt=vt: _dot_nt(p, vt)))
    o_s = _merge_parts(parts)

    wb = wbuf_ref.shape[-1]
    win_new = _pad_rows(win_new_ref[...], NEW_PAD)
    kw_new, vw_new = win_new[:, :NSA_HD].astype(BF16), win_new[:, NSA_HD:].astype(BF16)
    s_n = _dot_nt(q, kw_new)
    vwt = wbuf_ref[1].astype(BF16)
    s_b = _dot(q, wbuf_ref[0].astype(BF16))
    o_w = _merge_parts([
        _softmax_part(s_n, lambda p: _dot(p, vw_new), _iota(s_n.shape, 1) <= tok),
        _softmax_part(s_b, lambda p: _dot_nt(p, vwt), _iota(s_b.shape, 1) > tok + (wb - WINDOW))])
    _nsa_combine(gate_ref[...], o_c, o_s, o_w, o_ref, t)


def _pool_matrix(past, chunk):
    nsp = past // SEL_BLOCK
    blk = np.arange(past) // CMP_BLOCK
    col = np.where(blk % 2 == 0, blk // 2, nsp + blk // 2)
    p = np.zeros((past, past // CMP_BLOCK), np.float32)
    p[np.arange(past), col] = 1.0 / CMP_BLOCK
    return jnp.asarray(p.reshape(past // chunk, chunk, past // CMP_BLOCK), BF16)


def _nsa_sample(nq, gates, kv_sel, kv_win, win_t, cmp_pool, sel_pool, page_table, layer, row0, t, *, chunk=1024):
    batch, n_pages = page_table.shape
    assert t & (t - 1) == 0 and t <= NEW_PAD and t < CMP_BLOCK
    past = n_pages * PAGE_SIZE
    chunk = min(chunk, past)
    blk0 = row0 // t
    wb = win_t.shape[-1]
    pool = _pool_matrix(past, chunk)
    expand = _expand_matrix(chunk // SEL_BLOCK, chunk, chunk)[0]
    tok = lambda i, pt: (blk0 + i, 0)
    grid_spec = pltpu.PrefetchScalarGridSpec(
        num_scalar_prefetch=1,
        grid=(batch,),
        in_specs=[
            pl.BlockSpec((t, NSA_WIDTH), tok),
            pl.BlockSpec((t, LANES), tok),
            pl.BlockSpec((t, LANES), tok),
            pl.BlockSpec((t, LANES), tok),
            pl.BlockSpec((None, None, 2, NSA_HD, wb), lambda i, pt: (layer, i, 0, 0, 0)),
            pl.BlockSpec(pool.shape, lambda i, pt: (0, 0, 0)),
            pl.BlockSpec(expand.shape, lambda i, pt: (0, 0)),
            pl.BlockSpec(memory_space=pl.ANY),
            pl.BlockSpec(memory_space=pl.ANY),
        ],
        out_specs=pl.BlockSpec((t, NSA_WIDTH), lambda i, pt: (i, 0)),
        scratch_shapes=[pltpu.VMEM((2, 2, NSA_HD, past), F32), pltpu.VMEM((2, 2, NSA_HD, past), F32),
                        pltpu.SemaphoreType.DMA((2, 2))],
    )
    return pl.pallas_call(
        functools.partial(_nsa_sample_body, layer=layer, t=t, n_pages=n_pages, chunk=chunk),
        grid_spec=grid_spec,
        out_shape=jax.ShapeDtypeStruct((batch * t, NSA_WIDTH), F32),
        compiler_params=_params("arbitrary"),
        name="nsa_sample",
    )(page_table, nq, gates, kv_sel, kv_win, win_t, pool, expand, cmp_pool, sel_pool)


def _pack_w_in(w_in):
    d_model = w_in.shape[1]
    cuts = np.cumsum((HG_HEADS * HG_DK, HG_HEADS * HG_DK, HG_WIDTH, HG_WIDTH, NSA_WIDTH, 2 * NSA_HD,
                      2 * NSA_HD, 2 * NSA_HD, 3 * NSA_HEADS, MLA_HEADS * (MLA_NOPE + MLA_ROPE),
                      KV_LORA, MLA_ROPE, N_BRANCH * d_model)).tolist()
    w = w_in.astype(BF16)
    depth = w.shape[0]
    w_hg = w[:, :, :cuts[3]]
    nsa = w[:, :, cuts[3]:cuts[7]]
    ngate = w[:, :, cuts[7]:cuts[8]]
    mq = w[:, :, cuts[8]:cuts[9]].reshape(depth, d_model, MLA_HEADS, MLA_NOPE + MLA_ROPE)
    mqn = mq[..., :MLA_NOPE].reshape(depth, d_model, MLA_HEADS * MLA_NOPE)
    mqr = mq[..., MLA_NOPE:].reshape(depth, d_model, MLA_HEADS * MLA_ROPE)
    mc = w[:, :, cuts[9]:cuts[10]]
    mkr = w[:, :, cuts[10]:cuts[11]]
    pad = lambda k: jnp.zeros((depth, d_model, k), BF16)
    w_nm = jnp.concatenate([nsa, ngate, pad(LANES - 3 * NSA_HEADS), mqn, mqr, mc, mkr,
                            pad(LANES - MLA_ROPE)], axis=-1)
    w_gate = w[:, :, cuts[11]:]
    return w_hg, w_nm, w_gate


def _rope_tables(pos):
    n = pos.shape[0]
    posf = pos.astype(F32)[:, None]

    def cs(half):
        inv_freq = ROPE_THETA ** (-jnp.arange(half, dtype=F32) / half)
        ang = posf * inv_freq[None, :]
        return jnp.cos(ang), jnp.sin(ang)

    cn, sn = cs(NSA_ROT // 2)
    zn = jnp.zeros_like(sn)
    rest = NSA_HD - NSA_ROT
    head = (jnp.concatenate([cn, cn, jnp.ones((n, rest), F32)], 1),
            jnp.concatenate([-sn, zn, jnp.zeros((n, rest), F32)], 1),
            jnp.concatenate([zn, sn, jnp.zeros((n, rest), F32)], 1))
    ident = (jnp.ones((n, NSA_HD), F32), jnp.zeros((n, NSA_HD), F32), jnp.zeros((n, NSA_HD), F32))
    q_tabs = [jnp.tile(t, (1, LANES // NSA_HD)) for t in head]
    kv_tabs = [jnp.concatenate([t, i], 1) for t, i in zip(head, ident)]
    cm, sm = cs(MLA_ROPE // 2)
    zm = jnp.zeros_like(sm)
    grp = (jnp.concatenate([cm, cm], 1), jnp.concatenate([-sm, zm], 1), jnp.concatenate([zm, sm], 1))
    m_tabs = [jnp.tile(t, (1, LANES // MLA_ROPE)) for t in grp]
    return jnp.stack(q_tabs + kv_tabs + m_tabs)


def kernel(x_prompt, x_sample, cache_cmp_kv, cache_sel_kv, cache_mla_latent, cache_mla_krope, cache_win_kv,
           state_hgrn, page_table, norm_g, w_ffn_gu, w_ffn_dn, w_in, hg_lb, hg_norm, mla_norm, w_uk, w_uv,
           w_br, w_out):
    batch, seq, d = x_prompt.shape
    bs, ts, _ = x_sample.shape
    depth = norm_g.shape[0]
    n_pool = cache_cmp_kv.shape[1]
    n_pages = page_table.shape[1]
    past = n_pages * PAGE_SIZE
    wb = cache_win_kv.shape[2]
    rp = batch * seq
    n = rp + bs * ts
    assert n % 512 == 0 and rp % 512 == 0 and seq % 512 == 0 and bs % (HG_ROWS // ts) == 0

    x = jnp.concatenate([x_prompt.reshape(rp, d), x_sample.reshape(bs * ts, d)], axis=0)
    pos = jnp.concatenate([jnp.tile(jnp.arange(seq, dtype=jnp.int32), batch),
                           jnp.tile(past + jnp.arange(ts, dtype=jnp.int32), bs)])
    tabs = _rope_tables(pos)
    ng = norm_g.reshape(depth, 6, 1, d)
    w_gu = w_ffn_gu.astype(BF16)
    w_dn = w_ffn_dn.astype(BF16)
    w_hg, w_nm, w_gate = _pack_w_in(w_in)
    wuk_t = jnp.transpose(w_uk, (0, 2, 3, 1)).astype(BF16)
    wuv_t = jnp.transpose(w_uv, (0, 2, 1, 3)).astype(BF16)
    w_brb = w_br.astype(BF16)
    w_outb = w_out.astype(BF16)
    lb_all = _hgrn_lower_bounds(hg_lb)[:, None, :]
    hg_gain = hg_norm[:, None, :]
    mla_gain = mla_norm[:, None, :]
    cmp_pool = jnp.transpose(cache_cmp_kv, (0, 1, 3, 4, 2))
    sel_pool = jnp.transpose(cache_sel_kv, (0, 1, 3, 4, 2))
    krt_pool = jnp.transpose(cache_mla_krope, (0, 1, 3, 2))
    win_t = jnp.transpose(cache_win_kv, (0, 1, 3, 4, 2))

    st_p, st_s = [], []
    for l in range(depth):
        x = _ffn(x, ng, w_gu, w_dn, l, 0)
        hg_raw = _proj_hg(x, ng, w_hg, l)
        nq, gates, kv_cmp, kv_sel, kv_win, mqn, mqr, c, kr = _proj_nm(x, ng, w_nm, tabs, mla_gain, l)
        q_lat = _mla_absorb(mqn, wuk_t, l)

        o_hg_p, s_p = _hgrn_prompt(hg_raw, lb_all, hg_gain, l, batch, seq)
        o_hg_s, s_s = _hgrn_sample(hg_raw, lb_all, hg_gain, state_hgrn, l, rp, bs, ts)
        means = _nsa_means(kv_cmp, batch, seq)
        o_nsa_p = _nsa_prompt(nq, gates, means, kv_sel, kv_win, batch, seq)
        o_nsa_s = _nsa_sample(nq, gates, kv_sel, kv_win, win_t, cmp_pool, sel_pool, page_table, l, rp, ts)
        o_mla_p = _mla_prompt(q_lat, mqr, c, kr, wuv_t, l, batch, seq)
        o_mla_s = _mla_sample(q_lat, mqr, c, kr, wuv_t, cache_mla_latent, krt_pool, page_table, l, rp, ts)

        o_hg = jnp.concatenate([o_hg_p, o_hg_s], axis=0)
        o_nsa = jnp.concatenate([o_nsa_p, o_nsa_s], axis=0)
        o_mla = jnp.concatenate([o_mla_p, o_mla_s], axis=0)
        y = _merge(x, ng, o_hg, o_nsa, o_mla, w_gate, w_brb, l)
        x = _out_proj(x, y, ng, w_outb, l)
        x = _ffn(x, ng, w_gu, w_dn, l, 1)

        kv4 = lambda a, b, t: a.reshape(b, t, 2, NSA_HD)
        wp = min(WINDOW, seq)
        st_p.append((kv4(kv_cmp[:rp], batch, seq), kv4(kv_sel[:rp], batch, seq),
                     kv4(kv_win[:rp], batch, seq)[:, seq - wp:], c[:rp].reshape(batch, seq, KV_LORA),
                     kr[:rp].reshape(batch, seq, MLA_ROPE), s_p))
        win_new_t = jnp.transpose(kv4(kv_win[rp:], bs, ts), (0, 2, 3, 1))
        w_all_t = jnp.concatenate([win_t[l][..., ts:], win_new_t], axis=-1)
        st_s.append((kv4(kv_cmp[rp:], bs, ts), kv4(kv_sel[rp:], bs, ts), jnp.transpose(w_all_t, (0, 3, 1, 2)),
                     c[rp:].reshape(bs, ts, KV_LORA), kr[rp:].reshape(bs, ts, MLA_ROPE), s_s))

    stack = lambda states, i: jnp.stack([s[i] for s in states])
    return ((x[:rp].reshape(batch, seq, d), x[rp:].reshape(bs, ts, d))
            + tuple(stack(st_p, i) for i in range(6)) + tuple(stack(st_s, i) for i in range(6)))
```

```python
import functools

import numpy as np
import jax
import jax.numpy as jnp
from jax import lax
from jax.experimental import pallas as pl
from jax.experimental.pallas import tpu as pltpu

F32 = jnp.float32
BF16 = jnp.bfloat16

PAGE_SIZE = 128
HG_HEADS = 4
HG_DK = 128
HG_DV = 128
HG_WIDTH = HG_HEADS * HG_DV
NSA_HEADS = 8
NSA_HD = 64
NSA_WIDTH = NSA_HEADS * NSA_HD
CMP_BLOCK = 32
SEL_BLOCK = 64
N_SEL = 16
WINDOW = 512
FORCED_IMPORTANCE = 1e4
MLA_HEADS = 8
MLA_NOPE = 64
MLA_ROPE = 32
MLA_VD = 64
KV_LORA = 256
MLA_WIDTH = MLA_HEADS * MLA_VD
ROPE_THETA = 500000.0
NSA_ROT = NSA_HD // 4
N_BRANCH = 3
Q_BLOCK = 128
EPS = 1e-6
NEG_INF = -1e30
ABSENT = -3e38
LOG2E = 1.4426950408889634
MLA_QSCALE = (MLA_NOPE + MLA_ROPE) ** -0.5 * LOG2E
NSA_QSCALE = NSA_HD ** -0.5 * LOG2E

LANES = 128
SUBLANES = 8
VMEM_LIMIT_BYTES = 56 * 1024 * 1024

PJ_NQ = 0
PJ_CMP = PJ_NQ + NSA_WIDTH
PJ_SEL = PJ_CMP + 2 * NSA_HD
PJ_WIN = PJ_SEL + 2 * NSA_HD
PJ_GATE = PJ_WIN + 2 * NSA_HD
PJ_MQN = PJ_GATE + LANES
PJ_MQR = PJ_MQN + MLA_HEADS * MLA_NOPE
PJ_MC = PJ_MQR + MLA_HEADS * MLA_ROPE
PJ_MKR = PJ_MC + KV_LORA
PJ_END = PJ_MKR + LANES


def _params(*sem):
    return pltpu.CompilerParams(dimension_semantics=sem, vmem_limit_bytes=VMEM_LIMIT_BYTES)


def _dot(a, b):
    return jnp.dot(a, b, preferred_element_type=F32)


def _dot_nt(a, b):
    return lax.dot_general(a, b, (((1,), (1,)), ((), ())), preferred_element_type=F32)


def _dot_exact(a, b):
    return jnp.dot(a, b, preferred_element_type=F32, precision=lax.Precision.HIGHEST)


def _rms(x, g):
    return x * lax.rsqrt(jnp.mean(x * x, axis=-1, keepdims=True) + EPS) * g


def _silu(x):
    return x * jax.nn.sigmoid(x)


def _iota(shape, dim):
    return lax.broadcasted_iota(jnp.int32, shape, dim)


def _ffn_body(x_ref, gpre_ref, gpost_ref, wg_ref, wu_ref, wd_ref, o_ref, xn_ref, acc_ref):
    f = pl.program_id(1)

    @pl.when(f == 0)
    def _():
        xn_ref[...] = _rms(x_ref[...], gpre_ref[...]).astype(BF16)
        acc_ref[...] = jnp.zeros_like(acc_ref)

    xn = xn_ref[...]
    a = _dot(xn, wg_ref[...])
    b = _dot(xn, wu_ref[...])
    h = (_silu(a) * b).astype(BF16)
    acc_ref[...] += _dot(h, wd_ref[...])

    @pl.when(f == pl.num_programs(1) - 1)
    def _():
        o_ref[...] = x_ref[...] + 0.5 * _rms(acc_ref[...], gpost_ref[...])


def _ffn(x, norm_g, w_gu, w_dn, layer, which, *, tm=768, tf=512):
    n, d = x.shape
    ff = w_dn.shape[2]
    tm = min(tm, n)
    tf = min(tf, ff)
    assert n % tm == 0 and ff % tf == 0
    nf = ff // tf
    g_pre, g_post = (0, 1) if which == 0 else (4, 5)
    return pl.pallas_call(
        _ffn_body,
        grid=(n // tm, nf),
        in_specs=[
            pl.BlockSpec((tm, d), lambda i, f: (i, 0)),
            pl.BlockSpec((None, None, 1, d), lambda i, f: (layer, g_pre, 0, 0)),
            pl.BlockSpec((None, None, 1, d), lambda i, f: (layer, g_post, 0, 0)),
            pl.BlockSpec((None, None, d, tf), lambda i, f: (layer, which, 0, f)),
            pl.BlockSpec((None, None, d, tf), lambda i, f: (layer, which, 0, f + nf)),
            pl.BlockSpec((None, None, tf, d), lambda i, f: (layer, which, f, 0)),
        ],
        out_specs=pl.BlockSpec((tm, d), lambda i, f: (i, 0)),
        out_shape=jax.ShapeDtypeStruct((n, d), F32),
        scratch_shapes=[pltpu.VMEM((tm, d), BF16), pltpu.VMEM((tm, d), F32)],
        compiler_params=_params("parallel", "arbitrary"),
        name="ffn",
    )(x, norm_g, norm_g, w_gu, w_gu, w_dn)


def _proj_hg_body(x_ref, g_ref, w_ref, o_ref, z_ref):
    @pl.when(pl.program_id(1) == 0)
    def _():
        z_ref[...] = _rms(x_ref[...], g_ref[...]).astype(BF16)

    o_ref[...] = _dot(z_ref[...], w_ref[...])


def _proj_hg(x, norm_g, w_hg, layer, *, tm=512, tn=2048):
    n, d = x.shape
    wn = w_hg.shape[2]
    tm = min(tm, n)
    tn = min(tn, wn)
    return pl.pallas_call(
        _proj_hg_body,
        grid=(n // tm, wn // tn),
        in_specs=[
            pl.BlockSpec((tm, d), lambda i, j: (i, 0)),
            pl.BlockSpec((None, None, 1, d), lambda i, j: (layer, 2, 0, 0)),
            pl.BlockSpec((None, d, tn), lambda i, j: (layer, 0, j)),
        ],
        out_specs=pl.BlockSpec((tm, tn), lambda i, j: (i, j)),
        out_shape=jax.ShapeDtypeStruct((n, wn), F32),
        scratch_shapes=[pltpu.VMEM((tm, d), BF16)],
        compiler_params=_params("parallel", "arbitrary"),
        name="proj_hg",
    )(x, norm_g, w_hg)


def _rope_lanes(x, tab_ref, base, shift):
    cos, sa, sb = tab_ref[base], tab_ref[base + 1], tab_ref[base + 2]
    return (x * cos + pltpu.roll(x, LANES - shift, 1) * sa + pltpu.roll(x, shift, 1) * sb)


def _proj_nm_body(x_ref, g_ref, w_ref, tab_ref, mg_ref,
                  nq_ref, gate_ref, cmp_ref, sel_ref, win_ref, mqn_ref, mqr_ref, c_ref, kr_ref):
    z = _rms(x_ref[...], g_ref[...]).astype(BF16)
    y = _dot(z, w_ref[...])
    nh = NSA_ROT // 2
    mh = MLA_ROPE // 2
    for c in range(NSA_WIDTH // LANES):
        nq_ref[:, c * LANES:(c + 1) * LANES] = _rope_lanes(
            y[:, PJ_NQ + c * LANES:PJ_NQ + (c + 1) * LANES], tab_ref, 0, nh)
    cmp_ref[...] = _rope_lanes(y[:, PJ_CMP:PJ_CMP + LANES], tab_ref, 3, nh)
    sel_ref[...] = _rope_lanes(y[:, PJ_SEL:PJ_SEL + LANES], tab_ref, 3, nh)
    win_ref[...] = _rope_lanes(y[:, PJ_WIN:PJ_WIN + LANES], tab_ref, 3, nh)
    gate_ref[...] = jax.nn.sigmoid(y[:, PJ_GATE:PJ_GATE + LANES])
    mqn_ref[...] = y[:, PJ_MQN:PJ_MQR]
    for c in range(MLA_HEADS * MLA_ROPE // LANES):
        mqr_ref[:, c * LANES:(c + 1) * LANES] = _rope_lanes(
            y[:, PJ_MQR + c * LANES:PJ_MQR + (c + 1) * LANES], tab_ref, 6, mh)
    c_ref[...] = _rms(y[:, PJ_MC:PJ_MKR], mg_ref[...])
    kr_ref[...] = _rope_lanes(y[:, PJ_MKR:PJ_END], tab_ref, 6, mh)[:, :MLA_ROPE]


def _proj_nm(x, norm_g, w_nm, tabs, mla_norm, layer, *, tm=256):
    n, d = x.shape
    tm = min(tm, n)
    row = lambda i: (i, 0)
    widths = (NSA_WIDTH, LANES, LANES, LANES, LANES, MLA_HEADS * MLA_NOPE, MLA_HEADS * MLA_ROPE,
              KV_LORA, MLA_ROPE)
    return pl.pallas_call(
        _proj_nm_body,
        grid=(n // tm,),
        in_specs=[
            pl.BlockSpec((tm, d), row),
            pl.BlockSpec((None, None, 1, d), lambda i: (layer, 2, 0, 0)),
            pl.BlockSpec((None, d, PJ_END), lambda i: (layer, 0, 0)),
            pl.BlockSpec((9, tm, LANES), lambda i: (0, i, 0)),
            pl.BlockSpec((None, 1, KV_LORA), lambda i: (layer, 0, 0)),
        ],
        out_specs=[pl.BlockSpec((tm, w), row) for w in widths],
        out_shape=[jax.ShapeDtypeStruct((n, w), F32) for w in widths],
        compiler_params=_params("parallel"),
        name="proj_nsa_mla",
    )(x, norm_g, w_nm, tabs, mla_norm)


def _absorb_body(mqn_ref, wuk_ref, o_ref, *, scale):
    for h in range(MLA_HEADS):
        q = mqn_ref[:, h * MLA_NOPE:(h + 1) * MLA_NOPE].astype(BF16)
        o_ref[:, h * KV_LORA:(h + 1) * KV_LORA] = _dot(q, wuk_ref[h]) * scale


def _mla_absorb(mqn, wuk_t, layer, *, tm=512):
    n = mqn.shape[0]
    tm = min(tm, n)
    return pl.pallas_call(
        functools.partial(_absorb_body, scale=MLA_QSCALE),
        grid=(n // tm,),
        in_specs=[
            pl.BlockSpec((tm, MLA_HEADS * MLA_NOPE), lambda i: (i, 0)),
            pl.BlockSpec((None, MLA_HEADS, MLA_NOPE, KV_LORA), lambda i: (layer, 0, 0, 0)),
        ],
        out_specs=pl.BlockSpec((tm, MLA_HEADS * KV_LORA), lambda i: (i, 0)),
        out_shape=jax.ShapeDtypeStruct((n, MLA_HEADS * KV_LORA), F32),
        compiler_params=_params("parallel"),
        name="mla_absorb",
    )(mqn, wuk_t)


def _merge_body(x_ref, g_ref, o0_ref, o1_ref, o2_ref, wg0_ref, wg1_ref, wg2_ref,
                wb0_ref, wb1_ref, wb2_ref, y_ref, z_ref):
    @pl.when(pl.program_id(1) == 0)
    def _():
        z_ref[...] = _rms(x_ref[...], g_ref[...]).astype(BF16)

    z = z_ref[...]
    y = jnp.zeros(y_ref.shape, F32)
    for o_ref, wg_ref, wb_ref in ((o0_ref, wg0_ref, wb0_ref), (o1_ref, wg1_ref, wb1_ref),
                                  (o2_ref, wg2_ref, wb2_ref)):
        gate = jax.nn.sigmoid(_dot(z, wg_ref[...]))
        y = y + gate * _dot(o_ref[...].astype(BF16), wb_ref[...])
    y_ref[...] = y.astype(BF16)


def _merge(x, norm_g, o_hg, o_nsa, o_mla, w_gate, w_br, layer, *, tm=512, tn=512):
    n, d = x.shape
    tm = min(tm, n)
    nt = d // tn
    wb = w_br.shape[1] // N_BRANCH
    row = lambda i, j: (i, 0)
    return pl.pallas_call(
        _merge_body,
        grid=(n // tm, nt),
        in_specs=[
            pl.BlockSpec((tm, d), row),
            pl.BlockSpec((None, None, 1, d), lambda i, j: (layer, 2, 0, 0)),
            pl.BlockSpec((tm, wb), row), pl.BlockSpec((tm, wb), row), pl.BlockSpec((tm, wb), row),
            pl.BlockSpec((None, d, tn), lambda i, j: (layer, 0, j)),
            pl.BlockSpec((None, d, tn), lambda i, j: (layer, 0, nt + j)),
            pl.BlockSpec((None, d, tn), lambda i, j: (layer, 0, 2 * nt + j)),
            pl.BlockSpec((None, wb, tn), lambda i, j: (layer, 0, j)),
            pl.BlockSpec((None, wb, tn), lambda i, j: (layer, 1, j)),
            pl.BlockSpec((None, wb, tn), lambda i, j: (layer, 2, j)),
        ],
        out_specs=pl.BlockSpec((tm, tn), lambda i, j: (i, j)),
        out_shape=jax.ShapeDtypeStruct((n, d), BF16),
        scratch_shapes=[pltpu.VMEM((tm, d), BF16)],
        compiler_params=_params("parallel", "arbitrary"),
        name="merge",
    )(x, norm_g, o_hg, o_nsa, o_mla, w_gate, w_gate, w_gate, w_br, w_br, w_br)


def _out_proj_body(x_ref, y_ref, g_ref, w_ref, o_ref):
    o_ref[...] = x_ref[...] + _rms(_dot(y_ref[...], w_ref[...]), g_ref[...])


def _out_proj(x, y, norm_g, w_out, layer, *, tm=512):
    n, d = x.shape
    tm = min(tm, n)
    row = lambda i: (i, 0)
    return pl.pallas_call(
        _out_proj_body,
        grid=(n // tm,),
        in_specs=[
            pl.BlockSpec((tm, d), row),
            pl.BlockSpec((tm, d), row),
            pl.BlockSpec((None, None, 1, d), lambda i: (layer, 3, 0, 0)),
            pl.BlockSpec((None, d, d), lambda i: (layer, 0, 0)),
        ],
        out_specs=pl.BlockSpec((tm, d), row),
        out_shape=jax.ShapeDtypeStruct((n, d), F32),
        compiler_params=_params("parallel"),
        name="out_proj",
    )(x, y, norm_g, w_out)


HG_ROWS = 128
HG_DIAG = SUBLANES


def _lb_body(p_ref, o_ref):
    p = p_ref[...]
    e = jnp.exp(p - jnp.max(p, axis=0, keepdims=True))
    sm = e / jnp.sum(e, axis=0, keepdims=True)
    run = jnp.zeros_like(sm[0:1])
    for l in range(p.shape[0]):
        run = run + sm[l:l + 1]
        o_ref[l:l + 1, :] = run - sm[0:1]


def _hgrn_lower_bounds(hg_lb):
    return pl.pallas_call(
        _lb_body,
        out_shape=jax.ShapeDtypeStruct(hg_lb.shape, F32),
        name="hgrn_lower_bounds",
    )(hg_lb)


def _hg_inputs(x, lb, h):
    w = HG_DK
    hq = x[:, h * w:(h + 1) * w]
    hf = x[:, HG_WIDTH + h * w:HG_WIDTH + (h + 1) * w]
    v = x[:, 2 * HG_WIDTH + h * w:2 * HG_WIDTH + (h + 1) * w]
    g = x[:, 3 * HG_WIDTH + h * w:3 * HG_WIDTH + (h + 1) * w]
    lbh = lb[:, h * w:(h + 1) * w]
    q = _silu(hq)
    log_f = jnp.log(lbh + (1.0 - lbh) * jax.nn.sigmoid(hf))
    k = (1.0 - lbh) * jax.nn.sigmoid(-hf)
    return q, log_f, k, v, g


def _group_last(b, size):
    n = b.shape[0] // size
    b3 = b.reshape(n, size, b.shape[1])
    return jnp.broadcast_to(b3[:, size - 1:size, :], b3.shape).reshape(b.shape)


def _group_mid(b, half):
    n = b.shape[0] // (2 * half)
    b3 = b.reshape(n, 2 * half, b.shape[1])
    return jnp.broadcast_to(b3[:, half - 1:half, :], b3.shape).reshape(b.shape)


def _diag_blocks(q, k, b, rows, cols):
    a = jnp.zeros((q.shape[0], q.shape[0]), F32)
    rin = rows % HG_DIAG
    for d in range(HG_DIAG):
        if d == 0:
            kd, bd = k, b
        else:
            kd, bd = pltpu.roll(k, d, 0), pltpu.roll(b, d, 0)
        w = jnp.sum(q * kd * jnp.exp(jnp.minimum(b - bd, 0.0)), axis=-1, keepdims=True)
        a = a + jnp.where((cols == rows - d) & (rin >= d), w, 0.0)
    return a


def _cross_blocks(q, k, b, rows, cols, top):
    a = jnp.zeros((q.shape[0], q.shape[0]), F32)
    half = top // 2
    while half >= HG_DIAG:
        mid = _group_mid(b, half)
        right = (rows % (2 * half)) >= half
        e = jnp.exp(-jnp.abs(b - mid))
        lf = jnp.where(right, q * e, 0.0).astype(BF16)
        rf = jnp.where(right, 0.0, k * e).astype(BF16)
        same = (rows // (2 * half)) == (cols // (2 * half))
        a = a + jnp.where(same, _dot_nt(lf, rf), 0.0)
        half //= 2
    return a


def _col_of_row(e_row, rows, cols):
    return jnp.sum(jnp.where(rows == cols, e_row, 0.0), axis=-1, keepdims=True)


def _hg_readout(o, g, gain):
    return _rms(o, gain) * _silu(g)


def _hgrn_prompt_body(x_ref, lb_ref, gain_ref, o_ref, s_out_ref, s_ref):
    c = pl.program_id(1)

    @pl.when(c == 0)
    def _():
        s_ref[...] = jnp.zeros_like(s_ref)

    n = HG_ROWS
    rows, cols = _iota((n, n), 0), _iota((n, n), 1)
    tri = (cols <= rows).astype(F32)
    x = x_ref[...]
    lb = lb_ref[...]
    for h in range(HG_HEADS):
        q, log_f, k, v, g = _hg_inputs(x, lb, h)
        b = _dot_exact(tri, log_f)
        a = _diag_blocks(q, k, b, rows, cols) + _cross_blocks(q, k, b, rows, cols, n)
        s = s_ref[h]
        vb = v.astype(BF16)
        o = _dot((q * jnp.exp(b)).astype(BF16), s.astype(BF16)) + _dot(a.astype(BF16), vb)
        b_last = b[n - 1:n, :]
        kp = k * jnp.exp(b_last - b)
        s_new = _col_of_row(jnp.exp(b_last), rows, cols) * s + _dot(kp.T.astype(BF16), vb)
        s_ref[h] = s_new
        o_ref[:, h * HG_DV:(h + 1) * HG_DV] = _hg_readout(o, g, gain_ref[...])

    @pl.when(c == pl.num_programs(1) - 1)
    def _():
        s_out_ref[...] = s_ref[...]


def _hgrn_prompt(hg_raw, lb, gain, layer, batch, seq):
    nc = seq // HG_ROWS
    return pl.pallas_call(
        _hgrn_prompt_body,
        grid=(batch, nc),
        in_specs=[
            pl.BlockSpec((HG_ROWS, 4 * HG_WIDTH), lambda b, c: (b * nc + c, 0)),
            pl.BlockSpec((None, 1, HG_WIDTH), lambda b, c: (layer, 0, 0)),
            pl.BlockSpec((None, 1, HG_DV), lambda b, c: (layer, 0, 0)),
        ],
        out_specs=[
            pl.BlockSpec((HG_ROWS, HG_WIDTH), lambda b, c: (b * nc + c, 0)),
            pl.BlockSpec((None, HG_HEADS, HG_DK, HG_DV), lambda b, c: (b, 0, 0, 0)),
        ],
        out_shape=[jax.ShapeDtypeStruct((batch * seq, HG_WIDTH), F32),
                   jax.ShapeDtypeStruct((batch, HG_HEADS, HG_DK, HG_DV), F32)],
        scratch_shapes=[pltpu.VMEM((HG_HEADS, HG_DK, HG_DV), F32)],
        compiler_params=_params("parallel", "arbitrary"),
        name="hgrn_prompt",
    )(hg_raw, lb, gain)


def _hgrn_sample_body(x_ref, lb_ref, gain_ref, s0_ref, o_ref, s_out_ref, *, t):
    n = HG_ROWS
    nseq = n // t
    rows, cols = _iota((n, n), 0), _iota((n, n), 1)
    tri = ((cols <= rows) & (rows // t == cols // t)).astype(F32)
    x = x_ref[...]
    lb = lb_ref[...]
    for h in range(HG_HEADS):
        q, log_f, k, v, g = _hg_inputs(x, lb, h)
        b = _dot_exact(tri, log_f)
        a = _diag_blocks(q, k, b, rows, cols)
        vb = v.astype(BF16)
        o = _dot(a.astype(BF16), vb)
        eb = jnp.exp(b)
        qe = q * eb
        b_last = _group_last(b, t)
        kpt = (k * jnp.exp(b_last - b)).T
        for i in range(nseq):
            mine = (rows // t) == i
            s0 = s0_ref[i, h]
            o = o + _dot(jnp.where(mine, qe, 0.0).astype(BF16), s0.astype(BF16))
            e_col = _col_of_row(eb[i * t + t - 1:i * t + t, :], rows, cols)
            upd = _dot(jnp.where((cols // t) == i, kpt, 0.0).astype(BF16), vb)
            s_out_ref[i, h] = e_col * s0 + upd
        o_ref[:, h * HG_DV:(h + 1) * HG_DV] = _hg_readout(o, g, gain_ref[...])


def _hgrn_sample(hg_raw, lb, gain, state, layer, row0, batch, t):
    assert t == HG_DIAG and HG_ROWS % t == 0
    nseq = HG_ROWS // t
    blk0 = row0 // HG_ROWS
    return pl.pallas_call(
        functools.partial(_hgrn_sample_body, t=t),
        grid=(batch // nseq,),
        in_specs=[
            pl.BlockSpec((HG_ROWS, 4 * HG_WIDTH), lambda i: (blk0 + i, 0)),
            pl.BlockSpec((None, 1, HG_WIDTH), lambda i: (layer, 0, 0)),
            pl.BlockSpec((None, 1, HG_DV), lambda i: (layer, 0, 0)),
            pl.BlockSpec((None, nseq, HG_HEADS, HG_DK, HG_DV), lambda i: (layer, i, 0, 0, 0)),
        ],
        out_specs=[
            pl.BlockSpec((HG_ROWS, HG_WIDTH), lambda i: (i, 0)),
            pl.BlockSpec((nseq, HG_HEADS, HG_DK, HG_DV), lambda i: (i, 0, 0, 0)),
        ],
        out_shape=[jax.ShapeDtypeStruct((batch * t, HG_WIDTH), F32),
                   jax.ShapeDtypeStruct((batch, HG_HEADS, HG_DK, HG_DV), F32)],
        compiler_params=_params("parallel"),
        name="hgrn_sample",
    )(hg_raw, lb, gain, state)


def _stack_heads(x, heads, width):
    return jnp.concatenate([x[:, h * width:(h + 1) * width] for h in range(heads)], axis=0)


def _mla_up(o_lat, wuv_ref, o_ref, t):
    for h in range(MLA_HEADS):
        ol = o_lat[h * t:(h + 1) * t, :].astype(BF16)
        o_ref[:, h * MLA_VD:(h + 1) * MLA_VD] = _dot(ol, wuv_ref[h])


def _mla_prompt_body(ql_in_ref, qr_in_ref, c_ref, kr_ref, wuv_ref, o_ref,
                     ql_ref, qr_ref, m_ref, l_ref, acc_ref, *, tq, tk, scale):
    i, j = pl.program_id(1), pl.program_id(2)
    q0, k0 = i * tq, j * tk
    last_j = (q0 + tq - 1) // tk

    @pl.when(j == 0)
    def _():
        ql_ref[...] = _stack_heads(ql_in_ref[...], MLA_HEADS, KV_LORA).astype(BF16)
        qr_ref[...] = (_stack_heads(qr_in_ref[...], MLA_HEADS, MLA_ROPE) * scale).astype(BF16)
        m_ref[...] = jnp.full_like(m_ref, NEG_INF)
        l_ref[...] = jnp.zeros_like(l_ref)
        acc_ref[...] = jnp.zeros_like(acc_ref)

    def step(causal):
        cb = c_ref[...].astype(BF16)
        kb = kr_ref[...].astype(BF16)
        s = _dot_nt(ql_ref[...], cb) + _dot_nt(qr_ref[...], kb)
        if causal:
            qpos = q0 + (_iota(s.shape, 0) & (tq - 1))
            s = jnp.where(k0 + _iota(s.shape, 1) <= qpos, s, NEG_INF)
        m_prev = m_ref[...]
        m_new = jnp.maximum(m_prev, jnp.max(s, axis=-1, keepdims=True))
        alpha = jnp.exp2(m_prev - m_new)
        p = jnp.exp2(s - m_new)
        l_ref[...] = alpha * l_ref[...] + jnp.sum(p, axis=-1, keepdims=True)
        acc_ref[...] = alpha * acc_ref[...] + _dot(p.astype(BF16), cb)
        m_ref[...] = m_new

    @pl.when(j < last_j)
    def _():
        step(False)

    @pl.when(j == last_j)
    def _():
        step(True)
        _mla_up(acc_ref[...] / l_ref[...], wuv_ref, o_ref, tq)


def _mla_prompt(q_lat, mqr, c, kr, wuv, layer, batch, seq, *, tq=128, tk=1024):
    tk = min(tk, seq)
    assert tk % tq == 0
    nq, nk = seq // tq, seq // tk
    scale = MLA_QSCALE
    hq = MLA_HEADS * tq

    def kv_map(b, i, j):
        return (b * nk + jnp.minimum(j, (i * tq + tq - 1) // tk), 0)

    return pl.pallas_call(
        functools.partial(_mla_prompt_body, tq=tq, tk=tk, scale=scale),
        grid=(batch, nq, nk),
        in_specs=[
            pl.BlockSpec((tq, MLA_HEADS * KV_LORA), lambda b, i, j: (b * nq + i, 0)),
            pl.BlockSpec((tq, MLA_HEADS * MLA_ROPE), lambda b, i, j: (b * nq + i, 0)),
            pl.BlockSpec((tk, KV_LORA), kv_map),
            pl.BlockSpec((tk, MLA_ROPE), kv_map),
            pl.BlockSpec((None, MLA_HEADS, KV_LORA, MLA_VD), lambda b, i, j: (layer, 0, 0, 0)),
        ],
        out_specs=pl.BlockSpec((tq, MLA_WIDTH), lambda b, i, j: (b * nq + i, 0)),
        out_shape=jax.ShapeDtypeStruct((batch * seq, MLA_WIDTH), F32),
        scratch_shapes=[pltpu.VMEM((hq, KV_LORA), BF16), pltpu.VMEM((hq, MLA_ROPE), BF16),
                        pltpu.VMEM((hq, 1), F32), pltpu.VMEM((hq, 1), F32),
                        pltpu.VMEM((hq, KV_LORA), F32)],
        compiler_params=_params("parallel", "parallel", "arbitrary"),
        name="mla_prompt",
    )(q_lat, mqr, c, kr, wuv)


NEW_PAD = 16


def _pad_rows(x, rows):
    return jnp.concatenate([x, jnp.zeros((rows - x.shape[0], x.shape[1]), x.dtype)], axis=0)


def _page_copies(pt_ref, pool_ref, buf_ref, sem_ref, layer, seq, slot, rows_last):
    def copy(p):
        rows = pl.ds(p * PAGE_SIZE, PAGE_SIZE)
        dst = buf_ref.at[(slot,) + (slice(None),) * (len(buf_ref.shape) - 2) + (rows,)] if rows_last \
            else buf_ref.at[slot, rows]
        return pltpu.make_async_copy(pool_ref.at[layer, pt_ref[seq, p]], dst, sem_ref.at[slot])
    return copy


def _fetch_pages(pt_ref, pools, sems, layer, n_pages, seq, slot, start):
    for k, (pool, buf, rows_last) in enumerate(pools):
        copy = _page_copies(pt_ref, pool, buf, sems.at[k], layer, seq, slot, rows_last)
        (_start_pages if start else _wait_pages)(copy, n_pages)


def _double_buffered_pages(pt_ref, pools, sems, layer, n_pages):
    b = pl.program_id(0)
    slot = b % 2

    @pl.when(b == 0)
    def _():
        _fetch_pages(pt_ref, pools, sems, layer, n_pages, 0, 0, True)

    @pl.when(b + 1 < pl.num_programs(0))
    def _():
        _fetch_pages(pt_ref, pools, sems, layer, n_pages, b + 1, 1 - slot, True)

    _fetch_pages(pt_ref, pools, sems, layer, n_pages, b, slot, False)
    return slot


def _softmax_part(s, pv, ok=None):
    if ok is not None:
        s = jnp.where(ok, s, NEG_INF)
    m = jnp.max(s, axis=-1, keepdims=True)
    p = jnp.exp2(s - m)
    if ok is not None:
        p = jnp.where(ok, p, 0.0)
    return m, jnp.sum(p, axis=-1, keepdims=True), pv(p.astype(BF16))


def _merge_parts(parts):
    m = parts[0][0]
    for part in parts[1:]:
        m = jnp.maximum(m, part[0])
    l, acc = 0.0, 0.0
    for mi, li, ai in parts:
        w = jnp.exp2(mi - m)
        l = l + w * li
        acc = acc + w * ai
    return acc / l


def _start_pages(copy, n_pages):
    def body(p, carry):
        copy(p).start()
        return carry
    lax.fori_loop(0, n_pages, body, 0)


def _wait_pages(copy, n_pages):
    def body(p, carry):
        copy(p).wait()
        return carry
    lax.fori_loop(0, n_pages, body, 0)


def _mla_sample_body(pt_ref, ql_in_ref, qr_in_ref, c_new_ref, kr_new_ref, wuv_ref, lat_hbm, krt_hbm,
                     o_ref, lat_buf, krt_buf, sems, *, layer, t, n_pages, chunk):
    pools = ((lat_hbm, lat_buf, False), (krt_hbm, krt_buf, True))
    slot = _double_buffered_pages(pt_ref, pools, sems, layer, n_pages)
    past = n_pages * PAGE_SIZE

    ql = _stack_heads(ql_in_ref[...], MLA_HEADS, KV_LORA).astype(BF16)
    qr = (_stack_heads(qr_in_ref[...], MLA_HEADS, MLA_ROPE) * MLA_QSCALE).astype(BF16)
    c_new = _pad_rows(c_new_ref[...], NEW_PAD).astype(BF16)
    kr_new = _pad_rows(kr_new_ref[...], NEW_PAD).astype(BF16)
    s_new = _dot_nt(ql, c_new) + _dot_nt(qr, kr_new)
    tok = _iota(s_new.shape, 0) & (t - 1)
    parts = [_softmax_part(s_new, lambda p: _dot(p, c_new), _iota(s_new.shape, 1) <= tok)]
    for ch in range(past // chunk):
        cb = lat_buf[slot, ch * chunk:(ch + 1) * chunk, :].astype(BF16)
        kb = krt_buf[slot, :, ch * chunk:(ch + 1) * chunk].astype(BF16)
        parts.append(_softmax_part(_dot_nt(ql, cb) + _dot(qr, kb), lambda p, cb=cb: _dot(p, cb)))
    _mla_up(_merge_parts(parts), wuv_ref, o_ref, t)


def _mla_sample(q_lat, mqr, c, kr, wuv, lat_pool, krt_pool, page_table, layer, row0, t, *, chunk=1024):
    batch, n_pages = page_table.shape
    assert t & (t - 1) == 0 and t <= NEW_PAD
    past = n_pages * PAGE_SIZE
    chunk = min(chunk, past)
    blk0 = row0 // t
    tok = lambda i, pt: (blk0 + i, 0)
    grid_spec = pltpu.PrefetchScalarGridSpec(
        num_scalar_prefetch=1,
        grid=(batch,),
        in_specs=[
            pl.BlockSpec((t, MLA_HEADS * KV_LORA), tok),
            pl.BlockSpec((t, MLA_HEADS * MLA_ROPE), tok),
            pl.BlockSpec((t, KV_LORA), tok),
            pl.BlockSpec((t, MLA_ROPE), tok),
            pl.BlockSpec((None, MLA_HEADS, KV_LORA, MLA_VD), lambda i, pt: (layer, 0, 0, 0)),
            pl.BlockSpec(memory_space=pl.ANY),
            pl.BlockSpec(memory_space=pl.ANY),
        ],
        out_specs=pl.BlockSpec((t, MLA_WIDTH), lambda i, pt: (i, 0)),
        scratch_shapes=[pltpu.VMEM((2, past, KV_LORA), F32), pltpu.VMEM((2, MLA_ROPE, past), F32),
                        pltpu.SemaphoreType.DMA((2, 2))],
    )
    return pl.pallas_call(
        functools.partial(_mla_sample_body, layer=layer, t=t, n_pages=n_pages, chunk=chunk),
        grid_spec=grid_spec,
        out_shape=jax.ShapeDtypeStruct((batch * t, MLA_WIDTH), F32),
        compiler_params=_params("arbitrary"),
        name="mla_sample",
    )(page_table, q_lat, mqr, c, kr, wuv, lat_pool, krt_pool)


def _block_means(x):
    pairs = x.reshape(x.shape[0] // SEL_BLOCK, SEL_BLOCK, x.shape[1])
    even = jnp.sum(pairs[:, :CMP_BLOCK, :], axis=1)
    odd = jnp.sum(pairs[:, CMP_BLOCK:, :], axis=1)
    return jnp.concatenate([even, odd], axis=0) / CMP_BLOCK


def _means_body(x_ref, o_ref):
    o_ref[...] = _block_means(x_ref[...])


def _nsa_means(kv_cmp, batch, seq):
    return pl.pallas_call(
        _means_body,
        grid=(batch,),
        in_specs=[pl.BlockSpec((seq, LANES), lambda b: (b, 0))],
        out_specs=pl.BlockSpec((None, seq // CMP_BLOCK, LANES), lambda b: (b, 0, 0)),
        out_shape=jax.ShapeDtypeStruct((batch, seq // CMP_BLOCK, LANES), F32),
        compiler_params=_params("parallel"),
        name="nsa_means",
    )(kv_cmp)


def _softmax_rows(s, mask):
    if mask is not None:
        s = jnp.where(mask, s, NEG_INF)
    m = jnp.max(s, axis=-1, keepdims=True)
    e = jnp.exp2(s - m)
    if mask is not None:
        e = jnp.where(mask, e, 0.0)
    l = jnp.sum(e, axis=-1, keepdims=True)
    return e / jnp.where(l > 0.0, l, 1.0)


def _select_blocks(imp, qpos, ns):
    blk = _iota(imp.shape, 1)
    cur = qpos // SEL_BLOCK
    forced = (blk == 0) | (blk == cur) | (blk == cur - 1)
    key = jnp.where(forced, imp + FORCED_IMPORTANCE, imp)
    key = jnp.where(blk * SEL_BLOCK <= qpos, key, NEG_INF)
    key = jnp.where(blk < ns, key, ABSENT)
    cnt = jnp.zeros(imp.shape, F32)
    for i in range(ns):
        col = key[:, i:i + 1]
        beats = (col > key) | ((col == key) & (blk > i))
        cnt = cnt + jnp.where(beats, 1.0, 0.0)
    return cnt < float(min(N_SEL, ns))


def _pad_lanes(x, lanes):
    if x.shape[1] == lanes:
        return x
    return jnp.concatenate([x, jnp.zeros((x.shape[0], lanes - x.shape[1]), x.dtype)], axis=1)


def _nsa_combine(gate, o_c, o_s, o_w, o_ref, t):
    for h in range(NSA_HEADS):
        r = slice(h * t, (h + 1) * t)
        o_ref[:, h * NSA_HD:(h + 1) * NSA_HD] = (gate[:, 3 * h:3 * h + 1] * o_c[r]
                                                 + gate[:, 3 * h + 1:3 * h + 2] * o_s[r]
                                                 + gate[:, 3 * h + 2:3 * h + 3] * o_w[r])


def _nsa_prompt_body(nq_ref, gate_ref, mean_ref, sel_ref, win_ref, exp_ref, o_ref, mp_ref, lp_ref, ap_ref,
                     *, tq, tk, seq, wlen):
    i = pl.program_id(1)
    q0 = i * tq
    rows = NSA_HEADS * tq
    q = (_stack_heads(nq_ref[...], NSA_HEADS, NSA_HD) * NSA_QSCALE).astype(BF16)
    tok_pos = q0 + (_iota((rows, 1), 0) & (tq - 1))
    qpos = q0 + _iota((tq, 1), 0)

    nc = seq // CMP_BLOCK
    ns = seq // SEL_BLOCK
    means = mean_ref[...]
    kc, vc = means[:, :NSA_HD].astype(BF16), means[:, NSA_HD:].astype(BF16)
    s_c = _dot_nt(q, kc)
    col = _iota(s_c.shape, 1)
    cid = jnp.where(col < ns, 2 * col, 2 * (col - ns) + 1)
    p_c = _softmax_rows(s_c, (cid + 1) * CMP_BLOCK - 1 <= tok_pos)
    o_c = _dot(p_c.astype(BF16), vc)
    imp = jnp.sum(p_c.reshape(NSA_HEADS, tq, nc), axis=0)
    lanes = -(-ns // LANES) * LANES
    imp = _pad_lanes(imp[:, :ns] + imp[:, ns:], lanes)
    chosen = _select_blocks(imp, qpos, ns) & (_iota((tq, lanes), 1) * SEL_BLOCK <= qpos)
    bias = jnp.where(chosen, 0.0, NEG_INF)

    nb = tk // SEL_BLOCK
    last_j = q0 // tk
    for j in range(seq // tk):
        def part(causal, j=j):
            kv = sel_ref[j * tk:(j + 1) * tk, :]
            ks, vs = kv[:, :NSA_HD].astype(BF16), kv[:, NSA_HD:].astype(BF16)
            tile_bias = _dot(bias[:, j * nb:(j + 1) * nb].astype(BF16), exp_ref[...])
            if causal:
                tile_bias = jnp.where(j * tk + _iota((tq, tk), 1) <= qpos, tile_bias, NEG_INF)
            s = (_dot_nt(q, ks).reshape(NSA_HEADS, tq, tk) + tile_bias[None]).reshape(rows, tk)
            mp_ref[j], lp_ref[j], ap_ref[j] = _softmax_part(s, lambda p: _dot(p, vs))

        @pl.when(j < last_j)
        def _():
            part(False)

        @pl.when(j == last_j)
        def _():
            part(True)

        @pl.when(j > last_j)
        def _():
            mp_ref[j] = jnp.full((rows, 1), NEG_INF, F32)
            lp_ref[j] = jnp.zeros((rows, 1), F32)
            ap_ref[j] = jnp.zeros((rows, NSA_HD), F32)

    o_s = _merge_parts([(mp_ref[j], lp_ref[j], ap_ref[j]) for j in range(seq // tk)])

    w0 = pl.multiple_of(jnp.clip(q0 + tq - wlen, 0, seq - wlen), SUBLANES)
    wkv = win_ref[pl.ds(w0, wlen), :]
    kw, vw = wkv[:, :NSA_HD].astype(BF16), wkv[:, NSA_HD:].astype(BF16)
    wpos = w0 + _iota((tq, wlen), 1)
    bias_w = jnp.where((wpos <= qpos) & (wpos > qpos - WINDOW), 0.0, NEG_INF)
    s_w = (_dot_nt(q, kw).reshape(NSA_HEADS, tq, wlen) + bias_w[None]).reshape(rows, wlen)
    _, l_w, a_w = _softmax_part(s_w, lambda p: _dot(p, vw))
    o_w = a_w / l_w
    _nsa_combine(gate_ref[...], o_c, o_s, o_w, o_ref, tq)


def _expand_matrix(lanes, keys, tile):
    e = (np.arange(keys)[None, :] // SEL_BLOCK) == np.arange(lanes)[:, None]
    e = e.reshape(lanes, keys // tile, tile).transpose(1, 0, 2)
    return jnp.asarray(e, BF16)


def _nsa_prompt(nq, gates, means, kv_sel, kv_win, batch, seq, *, tq=Q_BLOCK, tk=1024):
    tk = min(tk, seq)
    assert tk % tq == 0 and seq % tk == 0
    nqb = seq // tq
    nk = seq // tk
    rows = NSA_HEADS * tq
    wlen = min(WINDOW + tq, seq)
    expand = _expand_matrix(tk // SEL_BLOCK, tk, tk)[0]
    tok = lambda b, i: (b * nqb + i, 0)
    return pl.pallas_call(
        functools.partial(_nsa_prompt_body, tq=tq, tk=tk, seq=seq, wlen=wlen),
        grid=(batch, nqb),
        in_specs=[
            pl.BlockSpec((tq, NSA_WIDTH), tok),
            pl.BlockSpec((tq, LANES), tok),
            pl.BlockSpec((None, seq // CMP_BLOCK, LANES), lambda b, i: (b, 0, 0)),
            pl.BlockSpec((seq, LANES), lambda b, i: (b, 0)),
            pl.BlockSpec((seq, LANES), lambda b, i: (b, 0)),
            pl.BlockSpec(expand.shape, lambda b, i: (0, 0)),
        ],
        out_specs=pl.BlockSpec((tq, NSA_WIDTH), tok),
        out_shape=jax.ShapeDtypeStruct((batch * seq, NSA_WIDTH), F32),
        scratch_shapes=[pltpu.VMEM((nk, rows, 1), F32), pltpu.VMEM((nk, rows, 1), F32),
                        pltpu.VMEM((nk, rows, NSA_HD), F32)],
        compiler_params=_params("parallel", "parallel"),
        name="nsa_prompt",
    )(nq, gates, means, kv_sel, kv_win, expand)


def _nsa_sample_body(pt_ref, nq_ref, gate_ref, sel_new_ref, win_new_ref, wbuf_ref, pool_ref, exp_ref,
                     cmp_hbm, sel_hbm, o_ref, cmp_buf, sel_buf, sems, *, layer, t, n_pages, chunk):
    pools = ((cmp_hbm, cmp_buf, True), (sel_hbm, sel_buf, True))
    slot = _double_buffered_pages(pt_ref, pools, sems, layer, n_pages)
    past = n_pages * PAGE_SIZE
    rows = NSA_HEADS * t
    q = (_stack_heads(nq_ref[...], NSA_HEADS, NSA_HD) * NSA_QSCALE).astype(BF16)
    tok = _iota((rows, 1), 0) & (t - 1)

    ncp = past // CMP_BLOCK
    nsp = past // SEL_BLOCK
    ns = nsp + 1
    cw = chunk // CMP_BLOCK
    means = []
    for ch in range(past // chunk):
        x = cmp_buf[slot, :, :, ch * chunk:(ch + 1) * chunk].reshape(2 * NSA_HD, chunk)
        hi = x.astype(BF16)
        lo = (x - hi.astype(F32)).astype(BF16)
        pooled = _dot(jnp.concatenate([hi, lo], axis=0), pool_ref[...])
        means.append(pooled[:2 * NSA_HD] + pooled[2 * NSA_HD:])
    means = jnp.concatenate(means, axis=1)
    kct, vct = means[:NSA_HD].astype(BF16), means[NSA_HD:].astype(BF16)
    s_c = _dot(q, kct)
    e_c = jnp.exp2(s_c - jnp.max(s_c, axis=-1, keepdims=True))
    p_c = e_c / jnp.sum(e_c, axis=-1, keepdims=True)
    o_c = _dot_nt(p_c.astype(BF16), vct)
    imp = jnp.sum(p_c.reshape(NSA_HEADS, t, ncp), axis=0)
    imp = jnp.concatenate([imp[:, c * cw:c * cw + cw // 2] + imp[:, c * cw + cw // 2:(c + 1) * cw]
                           for c in range(past // chunk)], axis=1)
    lanes = -(-ns // LANES) * LANES
    imp = _pad_lanes(imp, lanes)
    chosen = _select_blocks(imp, past + _iota((t, 1), 0), ns)
    bias = jnp.concatenate([jnp.where(chosen, 0.0, NEG_INF)] * NSA_HEADS, axis=0)

    sel_new = _pad_rows(sel_new_ref[...], NEW_PAD)
    k_new, v_new = sel_new[:, :NSA_HD].astype(BF16), sel_new[:, NSA_HD:].astype(BF16)
    s_new = _dot_nt(q, k_new)
    ok_new = (_iota(s_new.shape, 1) <= tok) & (bias[:, nsp:nsp + 1] > -1.0)
    parts = [_softmax_part(s_new, lambda p: _dot(p, v_new), ok_new)]
    nb = chunk // SEL_BLOCK
    for ch in range(past // chunk):
        kt = sel_buf[slot, 0, :, ch * chunk:(ch + 1) * chunk].astype(BF16)
        vt = sel_buf[slot, 1, :, ch * chunk:(ch + 1) * chunk].astype(BF16)
        s = _dot(q, kt) + _dot(bias[:, ch * nb:(ch + 1) * nb].astype(BF16), exp_ref[...])
        parts.append(_softmax_part(s, lambda p, vt=vt: _dot_nt(p, vt)))
    o_s = _merge_parts(parts)

    wb = wbuf_ref.shape[-1]
    win_new = _pad_rows(win_new_ref[...], NEW_PAD)
    kw_new, vw_new = win_new[:, :NSA_HD].astype(BF16), win_new[:, NSA_HD:].astype(BF16)
    s_n = _dot_nt(q, kw_new)
    vwt = wbuf_ref[1].astype(BF16)
    s_b = _dot(q, wbuf_ref[0].astype(BF16))
    o_w = _merge_parts([
        _softmax_part(s_n, lambda p: _dot(p, vw_new), _iota(s_n.shape, 1) <= tok),
        _softmax_part(s_b, lambda p: _dot_nt(p, vwt), _iota(s_b.shape, 1) > tok + (wb - WINDOW))])
    _nsa_combine(gate_ref[...], o_c, o_s, o_w, o_ref, t)


def _pool_matrix(chunk):
    half = chunk // SEL_BLOCK
    blk = np.arange(chunk) // CMP_BLOCK
    col = np.where(blk % 2 == 0, blk // 2, half + blk // 2)
    p = np.zeros((chunk, chunk // CMP_BLOCK), np.float32)
    p[np.arange(chunk), col] = 1.0 / CMP_BLOCK
    return jnp.asarray(p, BF16)


def _nsa_sample(nq, gates, kv_sel, kv_win, win_t, cmp_pool, sel_pool, page_table, layer, row0, t, *, chunk=1024):
    batch, n_pages = page_table.shape
    assert t & (t - 1) == 0 and t <= NEW_PAD and t < CMP_BLOCK
    past = n_pages * PAGE_SIZE
    chunk = min(chunk, past)
    blk0 = row0 // t
    wb = win_t.shape[-1]
    pool = _pool_matrix(chunk)
    expand = _expand_matrix(chunk // SEL_BLOCK, chunk, chunk)[0]
    tok = lambda i, pt: (blk0 + i, 0)
    grid_spec = pltpu.PrefetchScalarGridSpec(
        num_scalar_prefetch=1,
        grid=(batch,),
        in_specs=[
            pl.BlockSpec((t, NSA_WIDTH), tok),
            pl.BlockSpec((t, LANES), tok),
            pl.BlockSpec((t, LANES), tok),
            pl.BlockSpec((t, LANES), tok),
            pl.BlockSpec((None, None, 2, NSA_HD, wb), lambda i, pt: (layer, i, 0, 0, 0)),
            pl.BlockSpec(pool.shape, lambda i, pt: (0, 0)),
            pl.BlockSpec(expand.shape, lambda i, pt: (0, 0)),
            pl.BlockSpec(memory_space=pl.ANY),
            pl.BlockSpec(memory_space=pl.ANY),
        ],
        out_specs=pl.BlockSpec((t, NSA_WIDTH), lambda i, pt: (i, 0)),
        scratch_shapes=[pltpu.VMEM((2, 2, NSA_HD, past), F32), pltpu.VMEM((2, 2, NSA_HD, past), F32),
                        pltpu.SemaphoreType.DMA((2, 2))],
    )
    return pl.pallas_call(
        functools.partial(_nsa_sample_body, layer=layer, t=t, n_pages=n_pages, chunk=chunk),
        grid_spec=grid_spec,
        out_shape=jax.ShapeDtypeStruct((batch * t, NSA_WIDTH), F32),
        compiler_params=_params("arbitrary"),
        name="nsa_sample",
    )(page_table, nq, gates, kv_sel, kv_win, win_t, pool, expand, cmp_pool, sel_pool)


def _pack_w_in(w_in):
    d_model = w_in.shape[1]
    cuts = np.cumsum((HG_HEADS * HG_DK, HG_HEADS * HG_DK, HG_WIDTH, HG_WIDTH, NSA_WIDTH, 2 * NSA_HD,
                      2 * NSA_HD, 2 * NSA_HD, 3 * NSA_HEADS, MLA_HEADS * (MLA_NOPE + MLA_ROPE),
                      KV_LORA, MLA_ROPE, N_BRANCH * d_model)).tolist()
    w = w_in.astype(BF16)
    depth = w.shape[0]
    w_hg = w[:, :, :cuts[3]]
    nsa = w[:, :, cuts[3]:cuts[7]]
    ngate = w[:, :, cuts[7]:cuts[8]]
    mq = w[:, :, cuts[8]:cuts[9]].reshape(depth, d_model, MLA_HEADS, MLA_NOPE + MLA_ROPE)
    mqn = mq[..., :MLA_NOPE].reshape(depth, d_model, MLA_HEADS * MLA_NOPE)
    mqr = mq[..., MLA_NOPE:].reshape(depth, d_model, MLA_HEADS * MLA_ROPE)
    mc = w[:, :, cuts[9]:cuts[10]]
    mkr = w[:, :, cuts[10]:cuts[11]]
    pad = lambda k: jnp.zeros((depth, d_model, k), BF16)
    w_nm = jnp.concatenate([nsa, ngate, pad(LANES - 3 * NSA_HEADS), mqn, mqr, mc, mkr,
                            pad(LANES - MLA_ROPE)], axis=-1)
    w_gate = w[:, :, cuts[11]:]
    return w_hg, w_nm, w_gate


def _rope_tables(pos):
    n = pos.shape[0]
    posf = pos.astype(F32)[:, None]

    def cs(half):
        inv_freq = ROPE_THETA ** (-jnp.arange(half, dtype=F32) / half)
        ang = posf * inv_freq[None, :]
        return jnp.cos(ang), jnp.sin(ang)

    cn, sn = cs(NSA_ROT // 2)
    zn = jnp.zeros_like(sn)
    rest = NSA_HD - NSA_ROT
    head = (jnp.concatenate([cn, cn, jnp.ones((n, rest), F32)], 1),
            jnp.concatenate([-sn, zn, jnp.zeros((n, rest), F32)], 1),
            jnp.concatenate([zn, sn, jnp.zeros((n, rest), F32)], 1))
    ident = (jnp.ones((n, NSA_HD), F32), jnp.zeros((n, NSA_HD), F32), jnp.zeros((n, NSA_HD), F32))
    q_tabs = [jnp.tile(t, (1, LANES // NSA_HD)) for t in head]
    kv_tabs = [jnp.concatenate([t, i], 1) for t, i in zip(head, ident)]
    cm, sm = cs(MLA_ROPE // 2)
    zm = jnp.zeros_like(sm)
    grp = (jnp.concatenate([cm, cm], 1), jnp.concatenate([-sm, zm], 1), jnp.concatenate([zm, sm], 1))
    m_tabs = [jnp.tile(t, (1, LANES // MLA_ROPE)) for t in grp]
    return jnp.stack(q_tabs + kv_tabs + m_tabs)


def kernel(x_prompt, x_sample, cache_cmp_kv, cache_sel_kv, cache_mla_latent, cache_mla_krope, cache_win_kv,
           state_hgrn, page_table, norm_g, w_ffn_gu, w_ffn_dn, w_in, hg_lb, hg_norm, mla_norm, w_uk, w_uv,
           w_br, w_out):
    batch, seq, d = x_prompt.shape
    bs, ts, _ = x_sample.shape
    depth = norm_g.shape[0]
    n_pool = cache_cmp_kv.shape[1]
    n_pages = page_table.shape[1]
    past = n_pages * PAGE_SIZE
    wb = cache_win_kv.shape[2]
    rp = batch * seq
    n = rp + bs * ts
    assert n % 512 == 0 and rp % 512 == 0 and seq % 512 == 0 and bs % (HG_ROWS // ts) == 0

    x = jnp.concatenate([x_prompt.reshape(rp, d), x_sample.reshape(bs * ts, d)], axis=0)
    pos = jnp.concatenate([jnp.tile(jnp.arange(seq, dtype=jnp.int32), batch),
                           jnp.tile(past + jnp.arange(ts, dtype=jnp.int32), bs)])
    tabs = _rope_tables(pos)
    ng = norm_g.reshape(depth, 6, 1, d)
    w_gu = w_ffn_gu.astype(BF16)
    w_dn = w_ffn_dn.astype(BF16)
    w_hg, w_nm, w_gate = _pack_w_in(w_in)
    wuk_t = jnp.transpose(w_uk, (0, 2, 3, 1)).astype(BF16)
    wuv_t = jnp.transpose(w_uv, (0, 2, 1, 3)).astype(BF16)
    w_brb = w_br.astype(BF16)
    w_outb = w_out.astype(BF16)
    lb_all = _hgrn_lower_bounds(hg_lb)[:, None, :]
    hg_gain = hg_norm[:, None, :]
    mla_gain = mla_norm[:, None, :]
    cmp_pool = jnp.transpose(cache_cmp_kv, (0, 1, 3, 4, 2))
    sel_pool = jnp.transpose(cache_sel_kv, (0, 1, 3, 4, 2))
    krt_pool = jnp.transpose(cache_mla_krope, (0, 1, 3, 2))
    win_t = jnp.transpose(cache_win_kv, (0, 1, 3, 4, 2))

    st_p, st_s = [], []
    for l in range(depth):
        x = _ffn(x, ng, w_gu, w_dn, l, 0)
        hg_raw = _proj_hg(x, ng, w_hg, l)
        nq, gates, kv_cmp, kv_sel, kv_win, mqn, mqr, c, kr = _proj_nm(x, ng, w_nm, tabs, mla_gain, l)
        q_lat = _mla_absorb(mqn, wuk_t, l)

        o_hg_p, s_p = _hgrn_prompt(hg_raw, lb_all, hg_gain, l, batch, seq)
        o_hg_s, s_s = _hgrn_sample(hg_raw, lb_all, hg_gain, state_hgrn, l, rp, bs, ts)
        means = _nsa_means(kv_cmp, batch, seq)
        o_nsa_p = _nsa_prompt(nq, gates, means, kv_sel, kv_win, batch, seq)
        o_nsa_s = _nsa_sample(nq, gates, kv_sel, kv_win, win_t, cmp_pool, sel_pool, page_table, l, rp, ts)
        o_mla_p = _mla_prompt(q_lat, mqr, c, kr, wuv_t, l, batch, seq)
        o_mla_s = _mla_sample(q_lat, mqr, c, kr, wuv_t, cache_mla_latent, krt_pool, page_table, l, rp, ts)

        o_hg = jnp.concatenate([o_hg_p, o_hg_s], axis=0)
        o_nsa = jnp.concatenate([o_nsa_p, o_nsa_s], axis=0)
        o_mla = jnp.concatenate([o_mla_p, o_mla_s], axis=0)
        y = _merge(x, ng, o_hg, o_nsa, o_mla, w_gate, w_brb, l)
        x = _out_proj(x, y, ng, w_outb, l)
        x = _ffn(x, ng, w_gu, w_dn, l, 1)

        kv4 = lambda a, b, t: a.reshape(b, t, 2, NSA_HD)
        wp = min(WINDOW, seq)
        st_p.append((kv4(kv_cmp[:rp], batch, seq), kv4(kv_sel[:rp], batch, seq),
                     kv4(kv_win[:rp], batch, seq)[:, seq - wp:], c[:rp].reshape(batch, seq, KV_LORA),
                     kr[:rp].reshape(batch, seq, MLA_ROPE), s_p))
        win_new_t = jnp.transpose(kv4(kv_win[rp:], bs, ts), (0, 2, 3, 1))
        w_all_t = jnp.concatenate([win_t[l][..., ts:], win_new_t], axis=-1)
        st_s.append((kv4(kv_cmp[rp:], bs, ts), kv4(kv_sel[rp:], bs, ts), jnp.transpose(w_all_t, (0, 3, 1, 2)),
                     c[rp:].reshape(bs, ts, KV_LORA), kr[rp:].reshape(bs, ts, MLA_ROPE), s_s))

    stack = lambda states, i: jnp.stack([s[i] for s in states])
    return ((x[:rp].reshape(batch, seq, d), x[rp:].reshape(bs, ts, d))
            + tuple(stack(st_p, i) for i in range(6)) + tuple(stack(st_s, i) for i in range(6)))
```

```python
import functools

import numpy as np
import jax
import jax.numpy as jnp
from jax import lax
from jax.experimental import pallas as pl
from jax.experimental.pallas import tpu as pltpu

F32 = jnp.float32
BF16 = jnp.bfloat16

PAGE_SIZE = 128
HG_HEADS = 4
HG_DK = 128
HG_DV = 128
HG_WIDTH = HG_HEADS * HG_DV
NSA_HEADS = 8
NSA_HD = 64
NSA_WIDTH = NSA_HEADS * NSA_HD
CMP_BLOCK = 32
SEL_BLOCK = 64
N_SEL = 16
WINDOW = 512
FORCED_IMPORTANCE = 1e4
MLA_HEADS = 8
MLA_NOPE = 64
MLA_ROPE = 32
MLA_VD = 64
KV_LORA = 256
MLA_WIDTH = MLA_HEADS * MLA_VD
ROPE_THETA = 500000.0
NSA_ROT = NSA_HD // 4
N_BRANCH = 3
Q_BLOCK = 128
EPS = 1e-6
NEG_INF = -1e30
ABSENT = -3e38
LOG2E = 1.4426950408889634
MLA_QSCALE = (MLA_NOPE + MLA_ROPE) ** -0.5 * LOG2E
NSA_QSCALE = NSA_HD ** -0.5 * LOG2E

LANES = 128
SUBLANES = 8
VMEM_LIMIT_BYTES = 56 * 1024 * 1024

PJ_NQ = 0
PJ_CMP = PJ_NQ + NSA_WIDTH
PJ_SEL = PJ_CMP + 2 * NSA_HD
PJ_WIN = PJ_SEL + 2 * NSA_HD
PJ_GATE = PJ_WIN + 2 * NSA_HD
PJ_MQN = PJ_GATE + LANES
PJ_MQR = PJ_MQN + MLA_HEADS * MLA_NOPE
PJ_MC = PJ_MQR + MLA_HEADS * MLA_ROPE
PJ_MKR = PJ_MC + KV_LORA
PJ_END = PJ_MKR + LANES


def _params(*sem):
    return pltpu.CompilerParams(dimension_semantics=sem, vmem_limit_bytes=VMEM_LIMIT_BYTES)


def _dot(a, b):
    return jnp.dot(a, b, preferred_element_type=F32)


def _dot_nt(a, b):
    return lax.dot_general(a, b, (((1,), (1,)), ((), ())), preferred_element_type=F32)


def _dot_exact(a, b):
    return jnp.dot(a, b, preferred_element_type=F32, precision=lax.Precision.HIGHEST)


def _rms(x, g):
    return x * lax.rsqrt(jnp.mean(x * x, axis=-1, keepdims=True) + EPS) * g


def _silu(x):
    return x * jax.nn.sigmoid(x)


def _iota(shape, dim):
    return lax.broadcasted_iota(jnp.int32, shape, dim)


def _ffn_body(x_ref, gpre_ref, gpost_ref, wg_ref, wu_ref, wd_ref, o_ref, xn_ref, acc_ref):
    f = pl.program_id(1)

    @pl.when(f == 0)
    def _():
        xn_ref[...] = _rms(x_ref[...], gpre_ref[...]).astype(BF16)
        acc_ref[...] = jnp.zeros_like(acc_ref)

    xn = xn_ref[...]
    a = _dot(xn, wg_ref[...])
    b = _dot(xn, wu_ref[...])
    h = (_silu(a) * b).astype(BF16)
    acc_ref[...] += _dot(h, wd_ref[...])

    @pl.when(f == pl.num_programs(1) - 1)
    def _():
        o_ref[...] = x_ref[...] + 0.5 * _rms(acc_ref[...], gpost_ref[...])


def _ffn(x, norm_g, w_gu, w_dn, layer, which, *, tm=768, tf=512):
    n, d = x.shape
    ff = w_dn.shape[2]
    tm = min(tm, n)
    tf = min(tf, ff)
    assert n % tm == 0 and ff % tf == 0
    nf = ff // tf
    g_pre, g_post = (0, 1) if which == 0 else (4, 5)
    return pl.pallas_call(
        _ffn_body,
        grid=(n // tm, nf),
        in_specs=[
            pl.BlockSpec((tm, d), lambda i, f: (i, 0)),
            pl.BlockSpec((None, None, 1, d), lambda i, f: (layer, g_pre, 0, 0)),
            pl.BlockSpec((None, None, 1, d), lambda i, f: (layer, g_post, 0, 0)),
            pl.BlockSpec((None, None, d, tf), lambda i, f: (layer, which, 0, f)),
            pl.BlockSpec((None, None, d, tf), lambda i, f: (layer, which, 0, f + nf)),
            pl.BlockSpec((None, None, tf, d), lambda i, f: (layer, which, f, 0)),
        ],
        out_specs=pl.BlockSpec((tm, d), lambda i, f: (i, 0)),
        out_shape=jax.ShapeDtypeStruct((n, d), F32),
        scratch_shapes=[pltpu.VMEM((tm, d), BF16), pltpu.VMEM((tm, d), F32)],
        compiler_params=_params("parallel", "arbitrary"),
        name="ffn",
    )(x, norm_g, norm_g, w_gu, w_gu, w_dn)


def _proj_hg_body(x_ref, g_ref, w_ref, o_ref, z_ref):
    @pl.when(pl.program_id(1) == 0)
    def _():
        z_ref[...] = _rms(x_ref[...], g_ref[...]).astype(BF16)

    o_ref[...] = _dot(z_ref[...], w_ref[...])


def _proj_hg(x, norm_g, w_hg, layer, *, tm=512, tn=2048):
    n, d = x.shape
    wn = w_hg.shape[2]
    tm = min(tm, n)
    tn = min(tn, wn)
    return pl.pallas_call(
        _proj_hg_body,
        grid=(n // tm, wn // tn),
        in_specs=[
            pl.BlockSpec((tm, d), lambda i, j: (i, 0)),
            pl.BlockSpec((None, None, 1, d), lambda i, j: (layer, 2, 0, 0)),
            pl.BlockSpec((None, d, tn), lambda i, j: (layer, 0, j)),
        ],
        out_specs=pl.BlockSpec((tm, tn), lambda i, j: (i, j)),
        out_shape=jax.ShapeDtypeStruct((n, wn), F32),
        scratch_shapes=[pltpu.VMEM((tm, d), BF16)],
        compiler_params=_params("parallel", "arbitrary"),
        name="proj_hg",
    )(x, norm_g, w_hg)


def _rope_lanes(x, tab_ref, base, shift):
    cos, sa, sb = tab_ref[base], tab_ref[base + 1], tab_ref[base + 2]
    return (x * cos + pltpu.roll(x, LANES - shift, 1) * sa + pltpu.roll(x, shift, 1) * sb)


def _proj_nm_body(x_ref, g_ref, w_ref, tab_ref, mg_ref,
                  nq_ref, gate_ref, cmp_ref, sel_ref, win_ref, mqn_ref, mqr_ref, c_ref, kr_ref):
    z = _rms(x_ref[...], g_ref[...]).astype(BF16)
    y = _dot(z, w_ref[...])
    nh = NSA_ROT // 2
    mh = MLA_ROPE // 2
    for c in range(NSA_WIDTH // LANES):
        nq_ref[:, c * LANES:(c + 1) * LANES] = _rope_lanes(
            y[:, PJ_NQ + c * LANES:PJ_NQ + (c + 1) * LANES], tab_ref, 0, nh)
    cmp_ref[...] = _rope_lanes(y[:, PJ_CMP:PJ_CMP + LANES], tab_ref, 3, nh)
    sel_ref[...] = _rope_lanes(y[:, PJ_SEL:PJ_SEL + LANES], tab_ref, 3, nh)
    win_ref[...] = _rope_lanes(y[:, PJ_WIN:PJ_WIN + LANES], tab_ref, 3, nh)
    gate_ref[...] = jax.nn.sigmoid(y[:, PJ_GATE:PJ_GATE + LANES])
    mqn_ref[...] = y[:, PJ_MQN:PJ_MQR]
    for c in range(MLA_HEADS * MLA_ROPE // LANES):
        mqr_ref[:, c * LANES:(c + 1) * LANES] = _rope_lanes(
            y[:, PJ_MQR + c * LANES:PJ_MQR + (c + 1) * LANES], tab_ref, 6, mh)
    c_ref[...] = _rms(y[:, PJ_MC:PJ_MKR], mg_ref[...])
    kr_ref[...] = _rope_lanes(y[:, PJ_MKR:PJ_END], tab_ref, 6, mh)[:, :MLA_ROPE]


def _proj_nm(x, norm_g, w_nm, tabs, mla_norm, layer, *, tm=256):
    n, d = x.shape
    tm = min(tm, n)
    row = lambda i: (i, 0)
    widths = (NSA_WIDTH, LANES, LANES, LANES, LANES, MLA_HEADS * MLA_NOPE, MLA_HEADS * MLA_ROPE,
              KV_LORA, MLA_ROPE)
    return pl.pallas_call(
        _proj_nm_body,
        grid=(n // tm,),
        in_specs=[
            pl.BlockSpec((tm, d), row),
            pl.BlockSpec((None, None, 1, d), lambda i: (layer, 2, 0, 0)),
            pl.BlockSpec((None, d, PJ_END), lambda i: (layer, 0, 0)),
            pl.BlockSpec((9, tm, LANES), lambda i: (0, i, 0)),
            pl.BlockSpec((None, 1, KV_LORA), lambda i: (layer, 0, 0)),
        ],
        out_specs=[pl.BlockSpec((tm, w), row) for w in widths],
        out_shape=[jax.ShapeDtypeStruct((n, w), F32) for w in widths],
        compiler_params=_params("parallel"),
        name="proj_nsa_mla",
    )(x, norm_g, w_nm, tabs, mla_norm)


def _absorb_body(mqn_ref, wuk_ref, o_ref, *, scale):
    for h in range(MLA_HEADS):
        q = mqn_ref[:, h * MLA_NOPE:(h + 1) * MLA_NOPE].astype(BF16)
        o_ref[:, h * KV_LORA:(h + 1) * KV_LORA] = _dot(q, wuk_ref[h]) * scale


def _mla_absorb(mqn, wuk_t, layer, *, tm=512):
    n = mqn.shape[0]
    tm = min(tm, n)
    return pl.pallas_call(
        functools.partial(_absorb_body, scale=MLA_QSCALE),
        grid=(n // tm,),
        in_specs=[
            pl.BlockSpec((tm, MLA_HEADS * MLA_NOPE), lambda i: (i, 0)),
            pl.BlockSpec((None, MLA_HEADS, MLA_NOPE, KV_LORA), lambda i: (layer, 0, 0, 0)),
        ],
        out_specs=pl.BlockSpec((tm, MLA_HEADS * KV_LORA), lambda i: (i, 0)),
        out_shape=jax.ShapeDtypeStruct((n, MLA_HEADS * KV_LORA), F32),
        compiler_params=_params("parallel"),
        name="mla_absorb",
    )(mqn, wuk_t)


def _merge_body(x_ref, g_ref, o0_ref, o1_ref, o2_ref, wg0_ref, wg1_ref, wg2_ref,
                wb0_ref, wb1_ref, wb2_ref, y_ref, z_ref):
    @pl.when(pl.program_id(1) == 0)
    def _():
        z_ref[...] = _rms(x_ref[...], g_ref[...]).astype(BF16)

    z = z_ref[...]
    y = jnp.zeros(y_ref.shape, F32)
    for o_ref, wg_ref, wb_ref in ((o0_ref, wg0_ref, wb0_ref), (o1_ref, wg1_ref, wb1_ref),
                                  (o2_ref, wg2_ref, wb2_ref)):
        gate = jax.nn.sigmoid(_dot(z, wg_ref[...]))
        y = y + gate * _dot(o_ref[...].astype(BF16), wb_ref[...])
    y_ref[...] = y.astype(BF16)


def _merge(x, norm_g, o_hg, o_nsa, o_mla, w_gate, w_br, layer, *, tm=512, tn=512):
    n, d = x.shape
    tm = min(tm, n)
    nt = d // tn
    wb = w_br.shape[1] // N_BRANCH
    row = lambda i, j: (i, 0)
    return pl.pallas_call(
        _merge_body,
        grid=(n // tm, nt),
        in_specs=[
            pl.BlockSpec((tm, d), row),
            pl.BlockSpec((None, None, 1, d), lambda i, j: (layer, 2, 0, 0)),
            pl.BlockSpec((tm, wb), row), pl.BlockSpec((tm, wb), row), pl.BlockSpec((tm, wb), row),
            pl.BlockSpec((None, d, tn), lambda i, j: (layer, 0, j)),
            pl.BlockSpec((None, d, tn), lambda i, j: (layer, 0, nt + j)),
            pl.BlockSpec((None, d, tn), lambda i, j: (layer, 0, 2 * nt + j)),
            pl.BlockSpec((None, wb, tn), lambda i, j: (layer, 0, j)),
            pl.BlockSpec((None, wb, tn), lambda i, j: (layer, 1, j)),
            pl.BlockSpec((None, wb, tn), lambda i, j: (layer, 2, j)),
        ],
        out_specs=pl.BlockSpec((tm, tn), lambda i, j: (i, j)),
        out_shape=jax.ShapeDtypeStruct((n, d), BF16),
        scratch_shapes=[pltpu.VMEM((tm, d), BF16)],
        compiler_params=_params("parallel", "arbitrary"),
        name="merge",
    )(x, norm_g, o_hg, o_nsa, o_mla, w_gate, w_gate, w_gate, w_br, w_br, w_br)


def _out_proj_body(x_ref, y_ref, g_ref, w_ref, o_ref):
    o_ref[...] = x_ref[...] + _rms(_dot(y_ref[...], w_ref[...]), g_ref[...])


def _out_proj(x, y, norm_g, w_out, layer, *, tm=512):
    n, d = x.shape
    tm = min(tm, n)
    row = lambda i: (i, 0)
    return pl.pallas_call(
        _out_proj_body,
        grid=(n // tm,),
        in_specs=[
            pl.BlockSpec((tm, d), row),
            pl.BlockSpec((tm, d), row),
            pl.BlockSpec((None, None, 1, d), lambda i: (layer, 3, 0, 0)),
            pl.BlockSpec((None, d, d), lambda i: (layer, 0, 0)),
        ],
        out_specs=pl.BlockSpec((tm, d), row),
        out_shape=jax.ShapeDtypeStruct((n, d), F32),
        compiler_params=_params("parallel"),
        name="out_proj",
    )(x, y, norm_g, w_out)


HG_ROWS = 128
HG_DIAG = SUBLANES


def _lb_body(p_ref, o_ref):
    p = p_ref[...]
    e = jnp.exp(p - jnp.max(p, axis=0, keepdims=True))
    sm = e / jnp.sum(e, axis=0, keepdims=True)
    run = jnp.zeros_like(sm[0:1])
    for l in range(p.shape[0]):
        run = run + sm[l:l + 1]
        o_ref[l:l + 1, :] = run - sm[0:1]


def _hgrn_lower_bounds(hg_lb):
    return pl.pallas_call(
        _lb_body,
        out_shape=jax.ShapeDtypeStruct(hg_lb.shape, F32),
        name="hgrn_lower_bounds",
    )(hg_lb)


def _hg_inputs(x, lb, h):
    w = HG_DK
    hq = x[:, h * w:(h + 1) * w]
    hf = x[:, HG_WIDTH + h * w:HG_WIDTH + (h + 1) * w]
    v = x[:, 2 * HG_WIDTH + h * w:2 * HG_WIDTH + (h + 1) * w]
    g = x[:, 3 * HG_WIDTH + h * w:3 * HG_WIDTH + (h + 1) * w]
    lbh = lb[:, h * w:(h + 1) * w]
    q = _silu(hq)
    log_f = jnp.log(lbh + (1.0 - lbh) * jax.nn.sigmoid(hf))
    k = (1.0 - lbh) * jax.nn.sigmoid(-hf)
    return q, log_f, k, v, g


def _group_last(b, size):
    n = b.shape[0] // size
    b3 = b.reshape(n, size, b.shape[1])
    return jnp.broadcast_to(b3[:, size - 1:size, :], b3.shape).reshape(b.shape)


def _group_mid(b, half):
    n = b.shape[0] // (2 * half)
    b3 = b.reshape(n, 2 * half, b.shape[1])
    return jnp.broadcast_to(b3[:, half - 1:half, :], b3.shape).reshape(b.shape)


def _diag_blocks(q, k, b, rows, cols):
    a = jnp.zeros((q.shape[0], q.shape[0]), F32)
    rin = rows % HG_DIAG
    for d in range(HG_DIAG):
        if d == 0:
            kd, bd = k, b
        else:
            kd, bd = pltpu.roll(k, d, 0), pltpu.roll(b, d, 0)
        w = jnp.sum(q * kd * jnp.exp(jnp.minimum(b - bd, 0.0)), axis=-1, keepdims=True)
        a = a + jnp.where((cols == rows - d) & (rin >= d), w, 0.0)
    return a


def _cross_blocks(q, k, b, rows, cols, top):
    a = jnp.zeros((q.shape[0], q.shape[0]), F32)
    half = top // 2
    while half >= HG_DIAG:
        mid = _group_mid(b, half)
        right = (rows % (2 * half)) >= half
        e = jnp.exp(-jnp.abs(b - mid))
        lf = jnp.where(right, q * e, 0.0).astype(BF16)
        rf = jnp.where(right, 0.0, k * e).astype(BF16)
        same = (rows // (2 * half)) == (cols // (2 * half))
        a = a + jnp.where(same, _dot_nt(lf, rf), 0.0)
        half //= 2
    return a


def _col_of_row(e_row, rows, cols):
    return jnp.sum(jnp.where(rows == cols, e_row, 0.0), axis=-1, keepdims=True)


def _hg_readout(o, g, gain):
    return _rms(o, gain) * _silu(g)


def _hgrn_prompt_body(x_ref, lb_ref, gain_ref, o_ref, s_out_ref, s_ref):
    c = pl.program_id(1)

    @pl.when(c == 0)
    def _():
        s_ref[...] = jnp.zeros_like(s_ref)

    n = HG_ROWS
    rows, cols = _iota((n, n), 0), _iota((n, n), 1)
    tri = (cols <= rows).astype(F32)
    x = x_ref[...]
    lb = lb_ref[...]
    for h in range(HG_HEADS):
        q, log_f, k, v, g = _hg_inputs(x, lb, h)
        b = _dot_exact(tri, log_f)
        a = _diag_blocks(q, k, b, rows, cols) + _cross_blocks(q, k, b, rows, cols, n)
        s = s_ref[h]
        vb = v.astype(BF16)
        o = _dot((q * jnp.exp(b)).astype(BF16), s.astype(BF16)) + _dot(a.astype(BF16), vb)
        b_last = b[n - 1:n, :]
        kp = k * jnp.exp(b_last - b)
        s_new = _col_of_row(jnp.exp(b_last), rows, cols) * s + _dot(kp.T.astype(BF16), vb)
        s_ref[h] = s_new
        o_ref[:, h * HG_DV:(h + 1) * HG_DV] = _hg_readout(o, g, gain_ref[...])

    @pl.when(c == pl.num_programs(1) - 1)
    def _():
        s_out_ref[...] = s_ref[...]


def _hgrn_prompt(hg_raw, lb, gain, layer, batch, seq):
    nc = seq // HG_ROWS
    return pl.pallas_call(
        _hgrn_prompt_body,
        grid=(batch, nc),
        in_specs=[
            pl.BlockSpec((HG_ROWS, 4 * HG_WIDTH), lambda b, c: (b * nc + c, 0)),
            pl.BlockSpec((None, 1, HG_WIDTH), lambda b, c: (layer, 0, 0)),
            pl.BlockSpec((None, 1, HG_DV), lambda b, c: (layer, 0, 0)),
        ],
        out_specs=[
            pl.BlockSpec((HG_ROWS, HG_WIDTH), lambda b, c: (b * nc + c, 0)),
            pl.BlockSpec((None, HG_HEADS, HG_DK, HG_DV), lambda b, c: (b, 0, 0, 0)),
        ],
        out_shape=[jax.ShapeDtypeStruct((batch * seq, HG_WIDTH), F32),
                   jax.ShapeDtypeStruct((batch, HG_HEADS, HG_DK, HG_DV), F32)],
        scratch_shapes=[pltpu.VMEM((HG_HEADS, HG_DK, HG_DV), F32)],
        compiler_params=_params("parallel", "arbitrary"),
        name="hgrn_prompt",
    )(hg_raw, lb, gain)


def _hgrn_sample_body(x_ref, lb_ref, gain_ref, s0_ref, o_ref, s_out_ref, *, t):
    n = HG_ROWS
    nseq = n // t
    rows, cols = _iota((n, n), 0), _iota((n, n), 1)
    tri = ((cols <= rows) & (rows // t == cols // t)).astype(F32)
    x = x_ref[...]
    lb = lb_ref[...]
    for h in range(HG_HEADS):
        q, log_f, k, v, g = _hg_inputs(x, lb, h)
        b = _dot_exact(tri, log_f)
        a = _diag_blocks(q, k, b, rows, cols)
        vb = v.astype(BF16)
        o = _dot(a.astype(BF16), vb)
        eb = jnp.exp(b)
        qe = q * eb
        b_last = _group_last(b, t)
        kpt = (k * jnp.exp(b_last - b)).T
        for i in range(nseq):
            mine = (rows // t) == i
            s0 = s0_ref[i, h]
            o = o + _dot(jnp.where(mine, qe, 0.0).astype(BF16), s0.astype(BF16))
            e_col = _col_of_row(eb[i * t + t - 1:i * t + t, :], rows, cols)
            upd = _dot(jnp.where((cols // t) == i, kpt, 0.0).astype(BF16), vb)
            s_out_ref[i, h] = e_col * s0 + upd
        o_ref[:, h * HG_DV:(h + 1) * HG_DV] = _hg_readout(o, g, gain_ref[...])


def _hgrn_sample(hg_raw, lb, gain, state, layer, row0, batch, t):
    assert t == HG_DIAG and HG_ROWS % t == 0
    nseq = HG_ROWS // t
    blk0 = row0 // HG_ROWS
    return pl.pallas_call(
        functools.partial(_hgrn_sample_body, t=t),
        grid=(batch // nseq,),
        in_specs=[
            pl.BlockSpec((HG_ROWS, 4 * HG_WIDTH), lambda i: (blk0 + i, 0)),
            pl.BlockSpec((None, 1, HG_WIDTH), lambda i: (layer, 0, 0)),
            pl.BlockSpec((None, 1, HG_DV), lambda i: (layer, 0, 0)),
            pl.BlockSpec((None, nseq, HG_HEADS, HG_DK, HG_DV), lambda i: (layer, i, 0, 0, 0)),
        ],
        out_specs=[
            pl.BlockSpec((HG_ROWS, HG_WIDTH), lambda i: (i, 0)),
            pl.BlockSpec((nseq, HG_HEADS, HG_DK, HG_DV), lambda i: (i, 0, 0, 0)),
        ],
        out_shape=[jax.ShapeDtypeStruct((batch * t, HG_WIDTH), F32),
                   jax.ShapeDtypeStruct((batch, HG_HEADS, HG_DK, HG_DV), F32)],
        compiler_params=_params("parallel"),
        name="hgrn_sample",
    )(hg_raw, lb, gain, state)


def _stack_heads(x, heads, width):
    return jnp.concatenate([x[:, h * width:(h + 1) * width] for h in range(heads)], axis=0)


def _mla_up(o_lat, wuv_ref, o_ref, t):
    for h in range(MLA_HEADS):
        ol = o_lat[h * t:(h + 1) * t, :].astype(BF16)
        o_ref[:, h * MLA_VD:(h + 1) * MLA_VD] = _dot(ol, wuv_ref[h])


def _mla_prompt_body(ql_in_ref, qr_in_ref, c_ref, kr_ref, wuv_ref, o_ref,
                     ql_ref, qr_ref, m_ref, l_ref, acc_ref, *, tq, tk, scale):
    i, j = pl.program_id(1), pl.program_id(2)
    q0, k0 = i * tq, j * tk
    last_j = (q0 + tq - 1) // tk

    @pl.when(j == 0)
    def _():
        ql_ref[...] = _stack_heads(ql_in_ref[...], MLA_HEADS, KV_LORA).astype(BF16)
        qr_ref[...] = (_stack_heads(qr_in_ref[...], MLA_HEADS, MLA_ROPE) * scale).astype(BF16)
        m_ref[...] = jnp.full_like(m_ref, NEG_INF)
        l_ref[...] = jnp.zeros_like(l_ref)
        acc_ref[...] = jnp.zeros_like(acc_ref)

    def step(causal):
        cb = c_ref[...].astype(BF16)
        kb = kr_ref[...].astype(BF16)
        s = _dot_nt(ql_ref[...], cb) + _dot_nt(qr_ref[...], kb)
        if causal:
            qpos = q0 + (_iota(s.shape, 0) & (tq - 1))
            s = jnp.where(k0 + _iota(s.shape, 1) <= qpos, s, NEG_INF)
        m_prev = m_ref[...]
        m_new = jnp.maximum(m_prev, jnp.max(s, axis=-1, keepdims=True))
        alpha = jnp.exp2(m_prev - m_new)
        p = jnp.exp2(s - m_new)
        l_ref[...] = alpha * l_ref[...] + jnp.sum(p, axis=-1, keepdims=True)
        acc_ref[...] = alpha * acc_ref[...] + _dot(p.astype(BF16), cb)
        m_ref[...] = m_new

    @pl.when(j < last_j)
    def _():
        step(False)

    @pl.when(j == last_j)
    def _():
        step(True)
        _mla_up(acc_ref[...] / l_ref[...], wuv_ref, o_ref, tq)


def _mla_prompt(q_lat, mqr, c, kr, wuv, layer, batch, seq, *, tq=128, tk=1024):
    tk = min(tk, seq)
    assert tk % tq == 0
    nq, nk = seq // tq, seq // tk
    scale = MLA_QSCALE
    hq = MLA_HEADS * tq

    def kv_map(b, i, j):
        return (b * nk + jnp.minimum(j, (i * tq + tq - 1) // tk), 0)

    return pl.pallas_call(
        functools.partial(_mla_prompt_body, tq=tq, tk=tk, scale=scale),
        grid=(batch, nq, nk),
        in_specs=[
            pl.BlockSpec((tq, MLA_HEADS * KV_LORA), lambda b, i, j: (b * nq + i, 0)),
            pl.BlockSpec((tq, MLA_HEADS * MLA_ROPE), lambda b, i, j: (b * nq + i, 0)),
            pl.BlockSpec((tk, KV_LORA), kv_map),
            pl.BlockSpec((tk, MLA_ROPE), kv_map),
            pl.BlockSpec((None, MLA_HEADS, KV_LORA, MLA_VD), lambda b, i, j: (layer, 0, 0, 0)),
        ],
        out_specs=pl.BlockSpec((tq, MLA_WIDTH), lambda b, i, j: (b * nq + i, 0)),
        out_shape=jax.ShapeDtypeStruct((batch * seq, MLA_WIDTH), F32),
        scratch_shapes=[pltpu.VMEM((hq, KV_LORA), BF16), pltpu.VMEM((hq, MLA_ROPE), BF16),
                        pltpu.VMEM((hq, 1), F32), pltpu.VMEM((hq, 1), F32),
                        pltpu.VMEM((hq, KV_LORA), F32)],
        compiler_params=_params("parallel", "parallel", "arbitrary"),
        name="mla_prompt",
    )(q_lat, mqr, c, kr, wuv)


NEW_PAD = 16


def _pad_rows(x, rows):
    return jnp.concatenate([x, jnp.zeros((rows - x.shape[0], x.shape[1]), x.dtype)], axis=0)


def _page_copies(pt_ref, pool_ref, buf_ref, sem_ref, layer, seq, slot, rows_last):
    def copy(p):
        rows = pl.ds(p * PAGE_SIZE, PAGE_SIZE)
        dst = buf_ref.at[(slot,) + (slice(None),) * (len(buf_ref.shape) - 2) + (rows,)] if rows_last \
            else buf_ref.at[slot, rows]
        return pltpu.make_async_copy(pool_ref.at[layer, pt_ref[seq, p]], dst, sem_ref.at[slot])
    return copy


def _fetch_pages(pt_ref, pools, sems, layer, n_pages, seq, slot, start):
    for k, (pool, buf, rows_last) in enumerate(pools):
        copy = _page_copies(pt_ref, pool, buf, sems.at[k], layer, seq, slot, rows_last)
        (_start_pages if start else _wait_pages)(copy, n_pages)


def _double_buffered_pages(pt_ref, pools, sems, layer, n_pages):
    b = pl.program_id(0)
    slot = b % 2

    @pl.when(b == 0)
    def _():
        _fetch_pages(pt_ref, pools, sems, layer, n_pages, 0, 0, True)

    @pl.when(b + 1 < pl.num_programs(0))
    def _():
        _fetch_pages(pt_ref, pools, sems, layer, n_pages, b + 1, 1 - slot, True)

    _fetch_pages(pt_ref, pools, sems, layer, n_pages, b, slot, False)
    return slot


def _softmax_part(s, pv, ok=None):
    if ok is not None:
        s = jnp.where(ok, s, NEG_INF)
    m = jnp.max(s, axis=-1, keepdims=True)
    p = jnp.exp2(s - m)
    if ok is not None:
        p = jnp.where(ok, p, 0.0)
    return m, jnp.sum(p, axis=-1, keepdims=True), pv(p.astype(BF16))


def _merge_parts(parts):
    m = parts[0][0]
    for part in parts[1:]:
        m = jnp.maximum(m, part[0])
    l, acc = 0.0, 0.0
    for mi, li, ai in parts:
        w = jnp.exp2(mi - m)
        l = l + w * li
        acc = acc + w * ai
    return acc / l


def _start_pages(copy, n_pages):
    def body(p, carry):
        copy(p).start()
        return carry
    lax.fori_loop(0, n_pages, body, 0)


def _wait_pages(copy, n_pages):
    def body(p, carry):
        copy(p).wait()
        return carry
    lax.fori_loop(0, n_pages, body, 0)


def _mla_sample_body(pt_ref, ql_in_ref, qr_in_ref, c_new_ref, kr_new_ref, wuv_ref, lat_hbm, krt_hbm,
                     o_ref, lat_buf, krt_buf, sems, *, layer, t, n_pages, chunk):
    pools = ((lat_hbm, lat_buf, False), (krt_hbm, krt_buf, True))
    slot = _double_buffered_pages(pt_ref, pools, sems, layer, n_pages)
    past = n_pages * PAGE_SIZE

    ql = _stack_heads(ql_in_ref[...], MLA_HEADS, KV_LORA).astype(BF16)
    qr = (_stack_heads(qr_in_ref[...], MLA_HEADS, MLA_ROPE) * MLA_QSCALE).astype(BF16)
    c_new = _pad_rows(c_new_ref[...], NEW_PAD).astype(BF16)
    kr_new = _pad_rows(kr_new_ref[...], NEW_PAD).astype(BF16)
    s_new = _dot_nt(ql, c_new) + _dot_nt(qr, kr_new)
    tok = _iota(s_new.shape, 0) & (t - 1)
    parts = [_softmax_part(s_new, lambda p: _dot(p, c_new), _iota(s_new.shape, 1) <= tok)]
    for ch in range(past // chunk):
        cb = lat_buf[slot, ch * chunk:(ch + 1) * chunk, :].astype(BF16)
        kb = krt_buf[slot, :, ch * chunk:(ch + 1) * chunk].astype(BF16)
        parts.append(_softmax_part(_dot_nt(ql, cb) + _dot(qr, kb), lambda p, cb=cb: _dot(p, cb)))
    _mla_up(_merge_parts(parts), wuv_ref, o_ref, t)


def _mla_sample(q_lat, mqr, c, kr, wuv, lat_pool, krt_pool, page_table, layer, row0, t, *, chunk=4096):
    batch, n_pages = page_table.shape
    assert t & (t - 1) == 0 and t <= NEW_PAD
    past = n_pages * PAGE_SIZE
    chunk = min(chunk, past)
    blk0 = row0 // t
    tok = lambda i, pt: (blk0 + i, 0)
    grid_spec = pltpu.PrefetchScalarGridSpec(
        num_scalar_prefetch=1,
        grid=(batch,),
        in_specs=[
            pl.BlockSpec((t, MLA_HEADS * KV_LORA), tok),
            pl.BlockSpec((t, MLA_HEADS * MLA_ROPE), tok),
            pl.BlockSpec((t, KV_LORA), tok),
            pl.BlockSpec((t, MLA_ROPE), tok),
            pl.BlockSpec((None, MLA_HEADS, KV_LORA, MLA_VD), lambda i, pt: (layer, 0, 0, 0)),
            pl.BlockSpec(memory_space=pl.ANY),
            pl.BlockSpec(memory_space=pl.ANY),
        ],
        out_specs=pl.BlockSpec((t, MLA_WIDTH), lambda i, pt: (i, 0)),
        scratch_shapes=[pltpu.VMEM((2, past, KV_LORA), F32), pltpu.VMEM((2, MLA_ROPE, past), F32),
                        pltpu.SemaphoreType.DMA((2, 2))],
    )
    return pl.pallas_call(
        functools.partial(_mla_sample_body, layer=layer, t=t, n_pages=n_pages, chunk=chunk),
        grid_spec=grid_spec,
        out_shape=jax.ShapeDtypeStruct((batch * t, MLA_WIDTH), F32),
        compiler_params=_params("arbitrary"),
        name="mla_sample",
    )(page_table, q_lat, mqr, c, kr, wuv, lat_pool, krt_pool)


def _block_means(x):
    pairs = x.reshape(x.shape[0] // SEL_BLOCK, SEL_BLOCK, x.shape[1])
    even = jnp.sum(pairs[:, :CMP_BLOCK, :], axis=1)
    odd = jnp.sum(pairs[:, CMP_BLOCK:, :], axis=1)
    return jnp.concatenate([even, odd], axis=0) / CMP_BLOCK


def _means_body(x_ref, o_ref):
    o_ref[...] = _block_means(x_ref[...])


def _nsa_means(kv_cmp, batch, seq):
    return pl.pallas_call(
        _means_body,
        grid=(batch,),
        in_specs=[pl.BlockSpec((seq, LANES), lambda b: (b, 0))],
        out_specs=pl.BlockSpec((None, seq // CMP_BLOCK, LANES), lambda b: (b, 0, 0)),
        out_shape=jax.ShapeDtypeStruct((batch, seq // CMP_BLOCK, LANES), F32),
        compiler_params=_params("parallel"),
        name="nsa_means",
    )(kv_cmp)


def _softmax_rows(s, mask):
    if mask is not None:
        s = jnp.where(mask, s, NEG_INF)
    m = jnp.max(s, axis=-1, keepdims=True)
    e = jnp.exp2(s - m)
    if mask is not None:
        e = jnp.where(mask, e, 0.0)
    l = jnp.sum(e, axis=-1, keepdims=True)
    return e / jnp.where(l > 0.0, l, 1.0)


def _select_blocks(imp, qpos, ns):
    blk = _iota(imp.shape, 1)
    cur = qpos // SEL_BLOCK
    forced = (blk == 0) | (blk == cur) | (blk == cur - 1)
    key = jnp.where(forced, imp + FORCED_IMPORTANCE, imp)
    key = jnp.where(blk * SEL_BLOCK <= qpos, key, NEG_INF)
    key = jnp.where(blk < ns, key, ABSENT)
    cnt = jnp.zeros(imp.shape, F32)
    for i in range(ns):
        col = key[:, i:i + 1]
        beats = (col > key) | ((col == key) & (blk > i))
        cnt = cnt + jnp.where(beats, 1.0, 0.0)
    return cnt < float(min(N_SEL, ns))


def _pad_lanes(x, lanes):
    if x.shape[1] == lanes:
        return x
    return jnp.concatenate([x, jnp.zeros((x.shape[0], lanes - x.shape[1]), x.dtype)], axis=1)


def _nsa_combine(gate, o_c, o_s, o_w, o_ref, t):
    for h in range(NSA_HEADS):
        r = slice(h * t, (h + 1) * t)
        o_ref[:, h * NSA_HD:(h + 1) * NSA_HD] = (gate[:, 3 * h:3 * h + 1] * o_c[r]
                                                 + gate[:, 3 * h + 1:3 * h + 2] * o_s[r]
                                                 + gate[:, 3 * h + 2:3 * h + 3] * o_w[r])


def _nsa_prompt_body(nq_ref, gate_ref, mean_ref, sel_ref, win_ref, exp_ref, o_ref, mp_ref, lp_ref, ap_ref,
                     *, tq, tk, seq, wlen):
    i = pl.program_id(1)
    q0 = i * tq
    rows = NSA_HEADS * tq
    q = (_stack_heads(nq_ref[...], NSA_HEADS, NSA_HD) * NSA_QSCALE).astype(BF16)
    tok_pos = q0 + (_iota((rows, 1), 0) & (tq - 1))
    qpos = q0 + _iota((tq, 1), 0)

    nc = seq // CMP_BLOCK
    ns = seq // SEL_BLOCK
    means = mean_ref[...]
    kc, vc = means[:, :NSA_HD].astype(BF16), means[:, NSA_HD:].astype(BF16)
    s_c = _dot_nt(q, kc)
    col = _iota(s_c.shape, 1)
    cid = jnp.where(col < ns, 2 * col, 2 * (col - ns) + 1)
    p_c = _softmax_rows(s_c, (cid + 1) * CMP_BLOCK - 1 <= tok_pos)
    o_c = _dot(p_c.astype(BF16), vc)
    imp = jnp.sum(p_c.reshape(NSA_HEADS, tq, nc), axis=0)
    lanes = -(-ns // LANES) * LANES
    imp = _pad_lanes(imp[:, :ns] + imp[:, ns:], lanes)
    chosen = _select_blocks(imp, qpos, ns) & (_iota((tq, lanes), 1) * SEL_BLOCK <= qpos)
    bias = jnp.where(chosen, 0.0, NEG_INF)

    nb = tk // SEL_BLOCK
    last_j = q0 // tk
    for j in range(seq // tk):
        def part(causal, j=j):
            kv = sel_ref[j * tk:(j + 1) * tk, :]
            ks, vs = kv[:, :NSA_HD].astype(BF16), kv[:, NSA_HD:].astype(BF16)
            tile_bias = _dot(bias[:, j * nb:(j + 1) * nb].astype(BF16), exp_ref[...])
            if causal:
                tile_bias = jnp.where(j * tk + _iota((tq, tk), 1) <= qpos, tile_bias, NEG_INF)
            s = (_dot_nt(q, ks).reshape(NSA_HEADS, tq, tk) + tile_bias[None]).reshape(rows, tk)
            mp_ref[j], lp_ref[j], ap_ref[j] = _softmax_part(s, lambda p: _dot(p, vs))

        @pl.when(j < last_j)
        def _():
            part(False)

        @pl.when(j == last_j)
        def _():
            part(True)

        @pl.when(j > last_j)
        def _():
            mp_ref[j] = jnp.full((rows, 1), NEG_INF, F32)
            lp_ref[j] = jnp.zeros((rows, 1), F32)
            ap_ref[j] = jnp.zeros((rows, NSA_HD), F32)

    o_s = _merge_parts([(mp_ref[j], lp_ref[j], ap_ref[j]) for j in range(seq // tk)])

    w0 = pl.multiple_of(jnp.clip(q0 + tq - wlen, 0, seq - wlen), SUBLANES)
    wkv = win_ref[pl.ds(w0, wlen), :]
    kw, vw = wkv[:, :NSA_HD].astype(BF16), wkv[:, NSA_HD:].astype(BF16)
    wpos = w0 + _iota((tq, wlen), 1)
    bias_w = jnp.where((wpos <= qpos) & (wpos > qpos - WINDOW), 0.0, NEG_INF)
    s_w = (_dot_nt(q, kw).reshape(NSA_HEADS, tq, wlen) + bias_w[None]).reshape(rows, wlen)
    _, l_w, a_w = _softmax_part(s_w, lambda p: _dot(p, vw))
    o_w = a_w / l_w
    _nsa_combine(gate_ref[...], o_c, o_s, o_w, o_ref, tq)


def _expand_matrix(lanes, keys, tile):
    e = (np.arange(keys)[None, :] // SEL_BLOCK) == np.arange(lanes)[:, None]
    e = e.reshape(lanes, keys // tile, tile).transpose(1, 0, 2)
    return jnp.asarray(e, BF16)


def _nsa_prompt(nq, gates, means, kv_sel, kv_win, batch, seq, *, tq=Q_BLOCK, tk=1024):
    tk = min(tk, seq)
    assert tk % tq == 0 and seq % tk == 0
    nqb = seq // tq
    nk = seq // tk
    rows = NSA_HEADS * tq
    wlen = min(WINDOW + tq, seq)
    expand = _expand_matrix(tk // SEL_BLOCK, tk, tk)[0]
    tok = lambda b, i: (b * nqb + i, 0)
    return pl.pallas_call(
        functools.partial(_nsa_prompt_body, tq=tq, tk=tk, seq=seq, wlen=wlen),
        grid=(batch, nqb),
        in_specs=[
            pl.BlockSpec((tq, NSA_WIDTH), tok),
            pl.BlockSpec((tq, LANES), tok),
            pl.BlockSpec((None, seq // CMP_BLOCK, LANES), lambda b, i: (b, 0, 0)),
            pl.BlockSpec((seq, LANES), lambda b, i: (b, 0)),
            pl.BlockSpec((seq, LANES), lambda b, i: (b, 0)),
            pl.BlockSpec(expand.shape, lambda b, i: (0, 0)),
        ],
        out_specs=pl.BlockSpec((tq, NSA_WIDTH), tok),
        out_shape=jax.ShapeDtypeStruct((batch * seq, NSA_WIDTH), F32),
        scratch_shapes=[pltpu.VMEM((nk, rows, 1), F32), pltpu.VMEM((nk, rows, 1), F32),
                        pltpu.VMEM((nk, rows, NSA_HD), F32)],
        compiler_params=_params("parallel", "parallel"),
        name="nsa_prompt",
    )(nq, gates, means, kv_sel, kv_win, expand)


def _nsa_sample_body(pt_ref, nq_ref, gate_ref, sel_new_ref, win_new_ref, wbuf_ref, pool_ref, exp_ref,
                     cmp_hbm, sel_hbm, o_ref, cmp_buf, sel_buf, sems, *, layer, t, n_pages, chunk):
    pools = ((cmp_hbm, cmp_buf, True), (sel_hbm, sel_buf, True))
    slot = _double_buffered_pages(pt_ref, pools, sems, layer, n_pages)
    past = n_pages * PAGE_SIZE
    rows = NSA_HEADS * t
    q = (_stack_heads(nq_ref[...], NSA_HEADS, NSA_HD) * NSA_QSCALE).astype(BF16)
    tok = _iota((rows, 1), 0) & (t - 1)

    ncp = past // CMP_BLOCK
    nsp = past // SEL_BLOCK
    ns = nsp + 1
    cw = chunk // CMP_BLOCK
    means = []
    for ch in range(past // chunk):
        x = cmp_buf[slot, :, :, ch * chunk:(ch + 1) * chunk].reshape(2 * NSA_HD, chunk)
        hi = x.astype(BF16)
        lo = (x - hi.astype(F32)).astype(BF16)
        pooled = _dot(jnp.concatenate([hi, lo], axis=0), pool_ref[...])
        means.append(pooled[:2 * NSA_HD] + pooled[2 * NSA_HD:])
    means = jnp.concatenate(means, axis=1)
    kct, vct = means[:NSA_HD].astype(BF16), means[NSA_HD:].astype(BF16)
    s_c = _dot(q, kct)
    e_c = jnp.exp2(s_c - jnp.max(s_c, axis=-1, keepdims=True))
    p_c = e_c / jnp.sum(e_c, axis=-1, keepdims=True)
    o_c = _dot_nt(p_c.astype(BF16), vct)
    imp = jnp.sum(p_c.reshape(NSA_HEADS, t, ncp), axis=0)
    imp = jnp.concatenate([imp[:, c * cw:c * cw + cw // 2] + imp[:, c * cw + cw // 2:(c + 1) * cw]
                           for c in range(past // chunk)], axis=1)
    lanes = -(-ns // LANES) * LANES
    imp = _pad_lanes(imp, lanes)
    chosen = _select_blocks(imp, past + _iota((t, 1), 0), ns)
    bias = jnp.concatenate([jnp.where(chosen, 0.0, NEG_INF)] * NSA_HEADS, axis=0)

    sel_new = _pad_rows(sel_new_ref[...], NEW_PAD)
    k_new, v_new = sel_new[:, :NSA_HD].astype(BF16), sel_new[:, NSA_HD:].astype(BF16)
    s_new = _dot_nt(q, k_new)
    ok_new = (_iota(s_new.shape, 1) <= tok) & (bias[:, nsp:nsp + 1] > -1.0)
    parts = [_softmax_part(s_new, lambda p: _dot(p, v_new), ok_new)]
    nb = chunk // SEL_BLOCK
    for ch in range(past // chunk):
        kt = sel_buf[slot, 0, :, ch * chunk:(ch + 1) * chunk].astype(BF16)
        vt = sel_buf[slot, 1, :, ch * chunk:(ch + 1) * chunk].astype(BF16)
        s = _dot(q, kt) + _dot(bias[:, ch * nb:(ch + 1) * nb].astype(BF16), exp_ref[...])
        parts.append(_softmax_part(s, lambda p, vt=vt: _dot_nt(p, vt)))
    o_s = _merge_parts(parts)

    wb = wbuf_ref.shape[-1]
    win_new = _pad_rows(win_new_ref[...], NEW_PAD)
    kw_new, vw_new = win_new[:, :NSA_HD].astype(BF16), win_new[:, NSA_HD:].astype(BF16)
    s_n = _dot_nt(q, kw_new)
    vwt = wbuf_ref[1].astype(BF16)
    s_b = _dot(q, wbuf_ref[0].astype(BF16))
    o_w = _merge_parts([
        _softmax_part(s_n, lambda p: _dot(p, vw_new), _iota(s_n.shape, 1) <= tok),
        _softmax_part(s_b, lambda p: _dot_nt(p, vwt), _iota(s_b.shape, 1) > tok + (wb - WINDOW))])
    _nsa_combine(gate_ref[...], o_c, o_s, o_w, o_ref, t)


def _pool_matrix(chunk):
    half = chunk // SEL_BLOCK
    blk = np.arange(chunk) // CMP_BLOCK
    col = np.where(blk % 2 == 0, blk // 2, half + blk // 2)
    p = np.zeros((chunk, chunk // CMP_BLOCK), np.float32)
    p[np.arange(chunk), col] = 1.0 / CMP_BLOCK
    return jnp.asarray(p, BF16)


def _nsa_sample(nq, gates, kv_sel, kv_win, win_t, cmp_pool, sel_pool, page_table, layer, row0, t, *, chunk=4096):
    batch, n_pages = page_table.shape
    assert t & (t - 1) == 0 and t <= NEW_PAD and t < CMP_BLOCK
    past = n_pages * PAGE_SIZE
    chunk = min(chunk, past)
    blk0 = row0 // t
    wb = win_t.shape[-1]
    pool = _pool_matrix(chunk)
    expand = _expand_matrix(chunk // SEL_BLOCK, chunk, chunk)[0]
    tok = lambda i, pt: (blk0 + i, 0)
    grid_spec = pltpu.PrefetchScalarGridSpec(
        num_scalar_prefetch=1,
        grid=(batch,),
        in_specs=[
            pl.BlockSpec((t, NSA_WIDTH), tok),
            pl.BlockSpec((t, LANES), tok),
            pl.BlockSpec((t, LANES), tok),
            pl.BlockSpec((t, LANES), tok),
            pl.BlockSpec((None, None, 2, NSA_HD, wb), lambda i, pt: (layer, i, 0, 0, 0)),
            pl.BlockSpec(pool.shape, lambda i, pt: (0, 0)),
            pl.BlockSpec(expand.shape, lambda i, pt: (0, 0)),
            pl.BlockSpec(memory_space=pl.ANY),
            pl.BlockSpec(memory_space=pl.ANY),
        ],
        out_specs=pl.BlockSpec((t, NSA_WIDTH), lambda i, pt: (i, 0)),
        scratch_shapes=[pltpu.VMEM((2, 2, NSA_HD, past), F32), pltpu.VMEM((2, 2, NSA_HD, past), F32),
                        pltpu.SemaphoreType.DMA((2, 2))],
    )
    return pl.pallas_call(
        functools.partial(_nsa_sample_body, layer=layer, t=t, n_pages=n_pages, chunk=chunk),
        grid_spec=grid_spec,
        out_shape=jax.ShapeDtypeStruct((batch * t, NSA_WIDTH), F32),
        compiler_params=_params("arbitrary"),
        name="nsa_sample",
    )(page_table, nq, gates, kv_sel, kv_win, win_t, pool, expand, cmp_pool, sel_pool)


def _pack_w_in(w_in):
    d_model = w_in.shape[1]
    cuts = np.cumsum((HG_HEADS * HG_DK, HG_HEADS * HG_DK, HG_WIDTH, HG_WIDTH, NSA_WIDTH, 2 * NSA_HD,
                      2 * NSA_HD, 2 * NSA_HD, 3 * NSA_HEADS, MLA_HEADS * (MLA_NOPE + MLA_ROPE),
                      KV_LORA, MLA_ROPE, N_BRANCH * d_model)).tolist()
    w = w_in.astype(BF16)
    depth = w.shape[0]
    w_hg = w[:, :, :cuts[3]]
    nsa = w[:, :, cuts[3]:cuts[7]]
    ngate = w[:, :, cuts[7]:cuts[8]]
    mq = w[:, :, cuts[8]:cuts[9]].reshape(depth, d_model, MLA_HEADS, MLA_NOPE + MLA_ROPE)
    mqn = mq[..., :MLA_NOPE].reshape(depth, d_model, MLA_HEADS * MLA_NOPE)
    mqr = mq[..., MLA_NOPE:].reshape(depth, d_model, MLA_HEADS * MLA_ROPE)
    mc = w[:, :, cuts[9]:cuts[10]]
    mkr = w[:, :, cuts[10]:cuts[11]]
    pad = lambda k: jnp.zeros((depth, d_model, k), BF16)
    w_nm = jnp.concatenate([nsa, ngate, pad(LANES - 3 * NSA_HEADS), mqn, mqr, mc, mkr,
                            pad(LANES - MLA_ROPE)], axis=-1)
    w_gate = w[:, :, cuts[11]:]
    return w_hg, w_nm, w_gate


def _rope_tables(pos):
    n = pos.shape[0]
    posf = pos.astype(F32)[:, None]

    def cs(half):
        inv_freq = ROPE_THETA ** (-jnp.arange(half, dtype=F32) / half)
        ang = posf * inv_freq[None, :]
        return jnp.cos(ang), jnp.sin(ang)

    cn, sn = cs(NSA_ROT // 2)
    zn = jnp.zeros_like(sn)
    rest = NSA_HD - NSA_ROT
    head = (jnp.concatenate([cn, cn, jnp.ones((n, rest), F32)], 1),
            jnp.concatenate([-sn, zn, jnp.zeros((n, rest), F32)], 1),
            jnp.concatenate([zn, sn, jnp.zeros((n, rest), F32)], 1))
    ident = (jnp.ones((n, NSA_HD), F32), jnp.zeros((n, NSA_HD), F32), jnp.zeros((n, NSA_HD), F32))
    q_tabs = [jnp.tile(t, (1, LANES // NSA_HD)) for t in head]
    kv_tabs = [jnp.concatenate([t, i], 1) for t, i in zip(head, ident)]
    cm, sm = cs(MLA_ROPE // 2)
    zm = jnp.zeros_like(sm)
    grp = (jnp.concatenate([cm, cm], 1), jnp.concatenate([-sm, zm], 1), jnp.concatenate([zm, sm], 1))
    m_tabs = [jnp.tile(t, (1, LANES // MLA_ROPE)) for t in grp]
    return jnp.stack(q_tabs + kv_tabs + m_tabs)


def kernel(x_prompt, x_sample, cache_cmp_kv, cache_sel_kv, cache_mla_latent, cache_mla_krope, cache_win_kv,
           state_hgrn, page_table, norm_g, w_ffn_gu, w_ffn_dn, w_in, hg_lb, hg_norm, mla_norm, w_uk, w_uv,
           w_br, w_out):
    batch, seq, d = x_prompt.shape
    bs, ts, _ = x_sample.shape
    depth = norm_g.shape[0]
    n_pool = cache_cmp_kv.shape[1]
    n_pages = page_table.shape[1]
    past = n_pages * PAGE_SIZE
    wb = cache_win_kv.shape[2]
    rp = batch * seq
    n = rp + bs * ts
    assert n % 512 == 0 and rp % 512 == 0 and seq % 512 == 0 and bs % (HG_ROWS // ts) == 0

    x = jnp.concatenate([x_prompt.reshape(rp, d), x_sample.reshape(bs * ts, d)], axis=0)
    pos = jnp.concatenate([jnp.tile(jnp.arange(seq, dtype=jnp.int32), batch),
                           jnp.tile(past + jnp.arange(ts, dtype=jnp.int32), bs)])
    tabs = _rope_tables(pos)
    ng = norm_g.reshape(depth, 6, 1, d)
    w_gu = w_ffn_gu.astype(BF16)
    w_dn = w_ffn_dn.astype(BF16)
    w_hg, w_nm, w_gate = _pack_w_in(w_in)
    wuk_t = jnp.transpose(w_uk, (0, 2, 3, 1)).astype(BF16)
    wuv_t = jnp.transpose(w_uv, (0, 2, 1, 3)).astype(BF16)
    w_brb = w_br.astype(BF16)
    w_outb = w_out.astype(BF16)
    lb_all = _hgrn_lower_bounds(hg_lb)[:, None, :]
    hg_gain = hg_norm[:, None, :]
    mla_gain = mla_norm[:, None, :]
    cmp_pool = jnp.transpose(cache_cmp_kv, (0, 1, 3, 4, 2))
    sel_pool = jnp.transpose(cache_sel_kv, (0, 1, 3, 4, 2))
    krt_pool = jnp.transpose(cache_mla_krope, (0, 1, 3, 2))
    win_t = jnp.transpose(cache_win_kv, (0, 1, 3, 4, 2))

    st_p, st_s = [], []
    for l in range(depth):
        x = _ffn(x, ng, w_gu, w_dn, l, 0)
        hg_raw = _proj_hg(x, ng, w_hg, l)
        nq, gates, kv_cmp, kv_sel, kv_win, mqn, mqr, c, kr = _proj_nm(x, ng, w_nm, tabs, mla_gain, l)
        q_lat = _mla_absorb(mqn, wuk_t, l)

        o_hg_p, s_p = _hgrn_prompt(hg_raw, lb_all, hg_gain, l, batch, seq)
        o_hg_s, s_s = _hgrn_sample(hg_raw, lb_all, hg_gain, state_hgrn, l, rp, bs, ts)
        means = _nsa_means(kv_cmp, batch, seq)
        o_nsa_p = _nsa_prompt(nq, gates, means, kv_sel, kv_win, batch, seq)
        o_nsa_s = _nsa_sample(nq, gates, kv_sel, kv_win, win_t, cmp_pool, sel_pool, page_table, l, rp, ts)
        o_mla_p = _mla_prompt(q_lat, mqr, c, kr, wuv_t, l, batch, seq)
        o_mla_s = _mla_sample(q_lat, mqr, c, kr, wuv_t, cache_mla_latent, krt_pool, page_table, l, rp, ts)

        o_hg = jnp.concatenate([o_hg_p, o_hg_s], axis=0)
        o_nsa = jnp.concatenate([o_nsa_p, o_nsa_s], axis=0)
        o_mla = jnp.concatenate([o_mla_p, o_mla_s], axis=0)
        y = _merge(x, ng, o_hg, o_nsa, o_mla, w_gate, w_brb, l)
        x = _out_proj(x, y, ng, w_outb, l)
        x = _ffn(x, ng, w_gu, w_dn, l, 1)

        kv4 = lambda a, b, t: a.reshape(b, t, 2, NSA_HD)
        wp = min(WINDOW, seq)
        st_p.append((kv4(kv_cmp[:rp], batch, seq), kv4(kv_sel[:rp], batch, seq),
                     kv4(kv_win[:rp], batch, seq)[:, seq - wp:], c[:rp].reshape(batch, seq, KV_LORA),
                     kr[:rp].reshape(batch, seq, MLA_ROPE), s_p))
        win_new_t = jnp.transpose(kv4(kv_win[rp:], bs, ts), (0, 2, 3, 1))
        w_all_t = jnp.concatenate([win_t[l][..., ts:], win_new_t], axis=-1)
        st_s.append((kv4(kv_cmp[rp:], bs, ts), kv4(kv_sel[rp:], bs, ts), jnp.transpose(w_all_t, (0, 3, 1, 2)),
                     c[rp:].reshape(bs, ts, KV_LORA), kr[rp:].reshape(bs, ts, MLA_ROPE), s_s))

    stack = lambda states, i: jnp.stack([s[i] for s in states])
    return ((x[:rp].reshape(batch, seq, d), x[rp:].reshape(bs, ts, d))
            + tuple(stack(st_p, i) for i in range(6)) + tuple(stack(st_s, i) for i in range(6)))
```

```python
import functools

import numpy as np
import jax
import jax.numpy as jnp
from jax import lax
from jax.experimental import pallas as pl
from jax.experimental.pallas import tpu as pltpu

F32 = jnp.float32
BF16 = jnp.bfloat16

PAGE_SIZE = 128
HG_HEADS = 4
HG_DK = 128
HG_DV = 128
HG_WIDTH = HG_HEADS * HG_DV
NSA_HEADS = 8
NSA_HD = 64
NSA_WIDTH = NSA_HEADS * NSA_HD
CMP_BLOCK = 32
SEL_BLOCK = 64
N_SEL = 16
WINDOW = 512
FORCED_IMPORTANCE = 1e4
MLA_HEADS = 8
MLA_NOPE = 64
MLA_ROPE = 32
MLA_VD = 64
KV_LORA = 256
MLA_WIDTH = MLA_HEADS * MLA_VD
ROPE_THETA = 500000.0
NSA_ROT = NSA_HD // 4
N_BRANCH = 3
Q_BLOCK = 128
EPS = 1e-6
NEG_INF = -1e30
ABSENT = -3e38
LOG2E = 1.4426950408889634
MLA_QSCALE = (MLA_NOPE + MLA_ROPE) ** -0.5 * LOG2E
NSA_QSCALE = NSA_HD ** -0.5 * LOG2E
MLA_ROW_GROUPS = 4
NSA_HEAD_GROUPS = 2

LANES = 128
SUBLANES = 8
VMEM_LIMIT_BYTES = 56 * 1024 * 1024

PJ_NQ = 0
PJ_CMP = PJ_NQ + NSA_WIDTH
PJ_SEL = PJ_CMP + 2 * NSA_HD
PJ_WIN = PJ_SEL + 2 * NSA_HD
PJ_GATE = PJ_WIN + 2 * NSA_HD
PJ_MQN = PJ_GATE + LANES
PJ_MQR = PJ_MQN + MLA_HEADS * MLA_NOPE
PJ_MC = PJ_MQR + MLA_HEADS * MLA_ROPE
PJ_MKR = PJ_MC + KV_LORA
PJ_END = PJ_MKR + LANES


def _params(*sem):
    return pltpu.CompilerParams(dimension_semantics=sem, vmem_limit_bytes=VMEM_LIMIT_BYTES)


def _dot(a, b):
    return jnp.dot(a, b, preferred_element_type=F32)


def _dot_nt(a, b):
    return lax.dot_general(a, b, (((1,), (1,)), ((), ())), preferred_element_type=F32)


def _dot_exact(a, b):
    return jnp.dot(a, b, preferred_element_type=F32, precision=lax.Precision.HIGHEST)


def _rms(x, g):
    return x * lax.rsqrt(jnp.mean(x * x, axis=-1, keepdims=True) + EPS) * g


def _silu(x):
    return x * jax.nn.sigmoid(x)


def _iota(shape, dim):
    return lax.broadcasted_iota(jnp.int32, shape, dim)


def _ffn_body(x_ref, gpre_ref, gpost_ref, wg_ref, wu_ref, wd_ref, o_ref, xn_ref, acc_ref):
    f = pl.program_id(1)

    @pl.when(f == 0)
    def _():
        xn_ref[...] = _rms(x_ref[...], gpre_ref[...]).astype(BF16)
        acc_ref[...] = jnp.zeros_like(acc_ref)

    xn = xn_ref[...]
    a = _dot(xn, wg_ref[...])
    b = _dot(xn, wu_ref[...])
    h = (_silu(a) * b).astype(BF16)
    acc_ref[...] += _dot(h, wd_ref[...])

    @pl.when(f == pl.num_programs(1) - 1)
    def _():
        o_ref[...] = x_ref[...] + 0.5 * _rms(acc_ref[...], gpost_ref[...])


def _ffn(x, norm_g, w_gu, w_dn, layer, which, *, tm=768, tf=512):
    n, d = x.shape
    ff = w_dn.shape[2]
    tm = min(tm, n)
    tf = min(tf, ff)
    assert n % tm == 0 and ff % tf == 0
    nf = ff // tf
    g_pre, g_post = (0, 1) if which == 0 else (4, 5)
    return pl.pallas_call(
        _ffn_body,
        grid=(n // tm, nf),
        in_specs=[
            pl.BlockSpec((tm, d), lambda i, f: (i, 0)),
            pl.BlockSpec((None, None, 1, d), lambda i, f: (layer, g_pre, 0, 0)),
            pl.BlockSpec((None, None, 1, d), lambda i, f: (layer, g_post, 0, 0)),
            pl.BlockSpec((None, None, d, tf), lambda i, f: (layer, which, 0, f)),
            pl.BlockSpec((None, None, d, tf), lambda i, f: (layer, which, 0, f + nf)),
            pl.BlockSpec((None, None, tf, d), lambda i, f: (layer, which, f, 0)),
        ],
        out_specs=pl.BlockSpec((tm, d), lambda i, f: (i, 0)),
        out_shape=jax.ShapeDtypeStruct((n, d), F32),
        scratch_shapes=[pltpu.VMEM((tm, d), BF16), pltpu.VMEM((tm, d), F32)],
        compiler_params=_params("parallel", "arbitrary"),
        name="ffn",
    )(x, norm_g, norm_g, w_gu, w_gu, w_dn)


def _proj_hg_body(x_ref, g_ref, w_ref, o_ref, z_ref):
    @pl.when(pl.program_id(1) == 0)
    def _():
        z_ref[...] = _rms(x_ref[...], g_ref[...]).astype(BF16)

    o_ref[...] = _dot(z_ref[...], w_ref[...])


def _proj_hg(x, norm_g, w_hg, layer, *, tm=512, tn=2048):
    n, d = x.shape
    wn = w_hg.shape[2]
    tm = min(tm, n)
    tn = min(tn, wn)
    return pl.pallas_call(
        _proj_hg_body,
        grid=(n // tm, wn // tn),
        in_specs=[
            pl.BlockSpec((tm, d), lambda i, j: (i, 0)),
            pl.BlockSpec((None, None, 1, d), lambda i, j: (layer, 2, 0, 0)),
            pl.BlockSpec((None, d, tn), lambda i, j: (layer, 0, j)),
        ],
        out_specs=pl.BlockSpec((tm, tn), lambda i, j: (i, j)),
        out_shape=jax.ShapeDtypeStruct((n, wn), F32),
        scratch_shapes=[pltpu.VMEM((tm, d), BF16)],
        compiler_params=_params("parallel", "arbitrary"),
        name="proj_hg",
    )(x, norm_g, w_hg)


def _rope_lanes(x, tab_ref, base, shift):
    cos, sa, sb = tab_ref[base], tab_ref[base + 1], tab_ref[base + 2]
    return (x * cos + pltpu.roll(x, LANES - shift, 1) * sa + pltpu.roll(x, shift, 1) * sb)


def _proj_nm_body(x_ref, g_ref, w_ref, tab_ref, mg_ref,
                  nq_ref, gate_ref, cmp_ref, sel_ref, win_ref, mqn_ref, mqr_ref, c_ref, kr_ref):
    z = _rms(x_ref[...], g_ref[...]).astype(BF16)
    y = _dot(z, w_ref[...])
    nh = NSA_ROT // 2
    mh = MLA_ROPE // 2
    for c in range(NSA_WIDTH // LANES):
        nq_ref[:, c * LANES:(c + 1) * LANES] = _rope_lanes(
            y[:, PJ_NQ + c * LANES:PJ_NQ + (c + 1) * LANES], tab_ref, 0, nh)
    cmp_ref[...] = _rope_lanes(y[:, PJ_CMP:PJ_CMP + LANES], tab_ref, 3, nh)
    sel_ref[...] = _rope_lanes(y[:, PJ_SEL:PJ_SEL + LANES], tab_ref, 3, nh)
    win_ref[...] = _rope_lanes(y[:, PJ_WIN:PJ_WIN + LANES], tab_ref, 3, nh)
    gate_ref[...] = jax.nn.sigmoid(y[:, PJ_GATE:PJ_GATE + LANES])
    mqn_ref[...] = y[:, PJ_MQN:PJ_MQR]
    for c in range(MLA_HEADS * MLA_ROPE // LANES):
        mqr_ref[:, c * LANES:(c + 1) * LANES] = _rope_lanes(
            y[:, PJ_MQR + c * LANES:PJ_MQR + (c + 1) * LANES], tab_ref, 6, mh)
    c_ref[...] = _rms(y[:, PJ_MC:PJ_MKR], mg_ref[...])
    kr_ref[...] = _rope_lanes(y[:, PJ_MKR:PJ_END], tab_ref, 6, mh)[:, :MLA_ROPE]


def _proj_nm(x, norm_g, w_nm, tabs, mla_norm, layer, *, tm=256):
    n, d = x.shape
    tm = min(tm, n)
    row = lambda i: (i, 0)
    widths = (NSA_WIDTH, LANES, LANES, LANES, LANES, MLA_HEADS * MLA_NOPE, MLA_HEADS * MLA_ROPE,
              KV_LORA, MLA_ROPE)
    return pl.pallas_call(
        _proj_nm_body,
        grid=(n // tm,),
        in_specs=[
            pl.BlockSpec((tm, d), row),
            pl.BlockSpec((None, None, 1, d), lambda i: (layer, 2, 0, 0)),
            pl.BlockSpec((None, d, PJ_END), lambda i: (layer, 0, 0)),
            pl.BlockSpec((9, tm, LANES), lambda i: (0, i, 0)),
            pl.BlockSpec((None, 1, KV_LORA), lambda i: (layer, 0, 0)),
        ],
        out_specs=[pl.BlockSpec((tm, w), row) for w in widths],
        out_shape=[jax.ShapeDtypeStruct((n, w), F32) for w in widths],
        compiler_params=_params("parallel"),
        name="proj_nsa_mla",
    )(x, norm_g, w_nm, tabs, mla_norm)


def _absorb_body(mqn_ref, wuk_ref, o_ref, *, scale):
    for h in range(MLA_HEADS):
        q = mqn_ref[:, h * MLA_NOPE:(h + 1) * MLA_NOPE].astype(BF16)
        o_ref[:, h * KV_LORA:(h + 1) * KV_LORA] = _dot(q, wuk_ref[h]) * scale


def _mla_absorb(mqn, wuk_t, layer, *, tm=512):
    n = mqn.shape[0]
    tm = min(tm, n)
    return pl.pallas_call(
        functools.partial(_absorb_body, scale=MLA_QSCALE),
        grid=(n // tm,),
        in_specs=[
            pl.BlockSpec((tm, MLA_HEADS * MLA_NOPE), lambda i: (i, 0)),
            pl.BlockSpec((None, MLA_HEADS, MLA_NOPE, KV_LORA), lambda i: (layer, 0, 0, 0)),
        ],
        out_specs=pl.BlockSpec((tm, MLA_HEADS * KV_LORA), lambda i: (i, 0)),
        out_shape=jax.ShapeDtypeStruct((n, MLA_HEADS * KV_LORA), F32),
        compiler_params=_params("parallel"),
        name="mla_absorb",
    )(mqn, wuk_t)


def _merge_body(x_ref, g_ref, o0_ref, o1_ref, o2_ref, wg0_ref, wg1_ref, wg2_ref,
                wb0_ref, wb1_ref, wb2_ref, y_ref, z_ref):
    @pl.when(pl.program_id(1) == 0)
    def _():
        z_ref[...] = _rms(x_ref[...], g_ref[...]).astype(BF16)

    z = z_ref[...]
    y = jnp.zeros(y_ref.shape, F32)
    for o_ref, wg_ref, wb_ref in ((o0_ref, wg0_ref, wb0_ref), (o1_ref, wg1_ref, wb1_ref),
                                  (o2_ref, wg2_ref, wb2_ref)):
        gate = jax.nn.sigmoid(_dot(z, wg_ref[...]))
        y = y + gate * _dot(o_ref[...].astype(BF16), wb_ref[...])
    y_ref[...] = y.astype(BF16)


def _merge(x, norm_g, o_hg, o_nsa, o_mla, w_gate, w_br, layer, *, tm=512, tn=512):
    n, d = x.shape
    tm = min(tm, n)
    nt = d // tn
    wb = w_br.shape[1] // N_BRANCH
    row = lambda i, j: (i, 0)
    return pl.pallas_call(
        _merge_body,
        grid=(n // tm, nt),
        in_specs=[
            pl.BlockSpec((tm, d), row),
            pl.BlockSpec((None, None, 1, d), lambda i, j: (layer, 2, 0, 0)),
            pl.BlockSpec((tm, wb), row), pl.BlockSpec((tm, wb), row), pl.BlockSpec((tm, wb), row),
            pl.BlockSpec((None, d, tn), lambda i, j: (layer, 0, j)),
            pl.BlockSpec((None, d, tn), lambda i, j: (layer, 0, nt + j)),
            pl.BlockSpec((None, d, tn), lambda i, j: (layer, 0, 2 * nt + j)),
            pl.BlockSpec((None, wb, tn), lambda i, j: (layer, 0, j)),
            pl.BlockSpec((None, wb, tn), lambda i, j: (layer, 1, j)),
            pl.BlockSpec((None, wb, tn), lambda i, j: (layer, 2, j)),
        ],
        out_specs=pl.BlockSpec((tm, tn), lambda i, j: (i, j)),
        out_shape=jax.ShapeDtypeStruct((n, d), BF16),
        scratch_shapes=[pltpu.VMEM((tm, d), BF16)],
        compiler_params=_params("parallel", "arbitrary"),
        name="merge",
    )(x, norm_g, o_hg, o_nsa, o_mla, w_gate, w_gate, w_gate, w_br, w_br, w_br)


def _out_proj_body(x_ref, y_ref, g_ref, w_ref, o_ref):
    o_ref[...] = x_ref[...] + _rms(_dot(y_ref[...], w_ref[...]), g_ref[...])


def _out_proj(x, y, norm_g, w_out, layer, *, tm=512):
    n, d = x.shape
    tm = min(tm, n)
    row = lambda i: (i, 0)
    return pl.pallas_call(
        _out_proj_body,
        grid=(n // tm,),
        in_specs=[
            pl.BlockSpec((tm, d), row),
            pl.BlockSpec((tm, d), row),
            pl.BlockSpec((None, None, 1, d), lambda i: (layer, 3, 0, 0)),
            pl.BlockSpec((None, d, d), lambda i: (layer, 0, 0)),
        ],
        out_specs=pl.BlockSpec((tm, d), row),
        out_shape=jax.ShapeDtypeStruct((n, d), F32),
        compiler_params=_params("parallel"),
        name="out_proj",
    )(x, y, norm_g, w_out)


HG_ROWS = 128
HG_DIAG = SUBLANES


def _lb_body(p_ref, o_ref):
    p = p_ref[...]
    e = jnp.exp(p - jnp.max(p, axis=0, keepdims=True))
    sm = e / jnp.sum(e, axis=0, keepdims=True)
    run = jnp.zeros_like(sm[0:1])
    for l in range(p.shape[0]):
        run = run + sm[l:l + 1]
        o_ref[l:l + 1, :] = run - sm[0:1]


def _hgrn_lower_bounds(hg_lb):
    return pl.pallas_call(
        _lb_body,
        out_shape=jax.ShapeDtypeStruct(hg_lb.shape, F32),
        name="hgrn_lower_bounds",
    )(hg_lb)


def _hg_inputs(x, lb, h):
    w = HG_DK
    hq = x[:, h * w:(h + 1) * w]
    hf = x[:, HG_WIDTH + h * w:HG_WIDTH + (h + 1) * w]
    v = x[:, 2 * HG_WIDTH + h * w:2 * HG_WIDTH + (h + 1) * w]
    g = x[:, 3 * HG_WIDTH + h * w:3 * HG_WIDTH + (h + 1) * w]
    lbh = lb[:, h * w:(h + 1) * w]
    q = _silu(hq)
    log_f = jnp.log(lbh + (1.0 - lbh) * jax.nn.sigmoid(hf))
    k = (1.0 - lbh) * jax.nn.sigmoid(-hf)
    return q, log_f, k, v, g


def _group_last(b, size):
    n = b.shape[0] // size
    b3 = b.reshape(n, size, b.shape[1])
    return jnp.broadcast_to(b3[:, size - 1:size, :], b3.shape).reshape(b.shape)


def _group_mid(b, half):
    n = b.shape[0] // (2 * half)
    b3 = b.reshape(n, 2 * half, b.shape[1])
    return jnp.broadcast_to(b3[:, half - 1:half, :], b3.shape).reshape(b.shape)


def _diag_blocks(q, k, b, rows, cols):
    a = jnp.zeros((q.shape[0], q.shape[0]), F32)
    rin = rows % HG_DIAG
    for d in range(HG_DIAG):
        if d == 0:
            kd, bd = k, b
        else:
            kd, bd = pltpu.roll(k, d, 0), pltpu.roll(b, d, 0)
        w = jnp.sum(q * kd * jnp.exp(jnp.minimum(b - bd, 0.0)), axis=-1, keepdims=True)
        a = a + jnp.where((cols == rows - d) & (rin >= d), w, 0.0)
    return a


def _cross_blocks(q, k, b, rows, cols, top):
    a = jnp.zeros((q.shape[0], q.shape[0]), F32)
    half = top // 2
    while half >= HG_DIAG:
        mid = _group_mid(b, half)
        right = (rows % (2 * half)) >= half
        e = jnp.exp(-jnp.abs(b - mid))
        lf = jnp.where(right, q * e, 0.0).astype(BF16)
        rf = jnp.where(right, 0.0, k * e).astype(BF16)
        same = (rows // (2 * half)) == (cols // (2 * half))
        a = a + jnp.where(same, _dot_nt(lf, rf), 0.0)
        half //= 2
    return a


def _col_of_row(e_row, rows, cols):
    return jnp.sum(jnp.where(rows == cols, e_row, 0.0), axis=-1, keepdims=True)


def _hg_readout(o, g, gain):
    return _rms(o, gain) * _silu(g)


def _hgrn_prompt_body(x_ref, lb_ref, gain_ref, o_ref, s_out_ref, s_ref):
    c = pl.program_id(1)

    @pl.when(c == 0)
    def _():
        s_ref[...] = jnp.zeros_like(s_ref)

    n = HG_ROWS
    rows, cols = _iota((n, n), 0), _iota((n, n), 1)
    tri = (cols <= rows).astype(F32)
    x = x_ref[...]
    lb = lb_ref[...]
    for h in range(HG_HEADS):
        q, log_f, k, v, g = _hg_inputs(x, lb, h)
        b = _dot_exact(tri, log_f)
        a = _diag_blocks(q, k, b, rows, cols) + _cross_blocks(q, k, b, rows, cols, n)
        s = s_ref[h]
        vb = v.astype(BF16)
        o = _dot((q * jnp.exp(b)).astype(BF16), s.astype(BF16)) + _dot(a.astype(BF16), vb)
        b_last = b[n - 1:n, :]
        kp = k * jnp.exp(b_last - b)
        s_new = _col_of_row(jnp.exp(b_last), rows, cols) * s + _dot(kp.T.astype(BF16), vb)
        s_ref[h] = s_new
        o_ref[:, h * HG_DV:(h + 1) * HG_DV] = _hg_readout(o, g, gain_ref[...])

    @pl.when(c == pl.num_programs(1) - 1)
    def _():
        s_out_ref[...] = s_ref[...]


def _hgrn_prompt(hg_raw, lb, gain, layer, batch, seq):
    nc = seq // HG_ROWS
    return pl.pallas_call(
        _hgrn_prompt_body,
        grid=(batch, nc),
        in_specs=[
            pl.BlockSpec((HG_ROWS, 4 * HG_WIDTH), lambda b, c: (b * nc + c, 0)),
            pl.BlockSpec((None, 1, HG_WIDTH), lambda b, c: (layer, 0, 0)),
            pl.BlockSpec((None, 1, HG_DV), lambda b, c: (layer, 0, 0)),
        ],
        out_specs=[
            pl.BlockSpec((HG_ROWS, HG_WIDTH), lambda b, c: (b * nc + c, 0)),
            pl.BlockSpec((None, HG_HEADS, HG_DK, HG_DV), lambda b, c: (b, 0, 0, 0)),
        ],
        out_shape=[jax.ShapeDtypeStruct((batch * seq, HG_WIDTH), F32),
                   jax.ShapeDtypeStruct((batch, HG_HEADS, HG_DK, HG_DV), F32)],
        scratch_shapes=[pltpu.VMEM((HG_HEADS, HG_DK, HG_DV), F32)],
        compiler_params=_params("parallel", "arbitrary"),
        name="hgrn_prompt",
    )(hg_raw, lb, gain)


def _hgrn_sample_body(x_ref, lb_ref, gain_ref, s0_ref, o_ref, s_out_ref, *, t):
    n = HG_ROWS
    nseq = n // t
    rows, cols = _iota((n, n), 0), _iota((n, n), 1)
    tri = ((cols <= rows) & (rows // t == cols // t)).astype(F32)
    x = x_ref[...]
    lb = lb_ref[...]
    for h in range(HG_HEADS):
        q, log_f, k, v, g = _hg_inputs(x, lb, h)
        b = _dot_exact(tri, log_f)
        a = _diag_blocks(q, k, b, rows, cols)
        vb = v.astype(BF16)
        o = _dot(a.astype(BF16), vb)
        eb = jnp.exp(b)
        qe = q * eb
        b_last = _group_last(b, t)
        kpt = (k * jnp.exp(b_last - b)).T
        for i in range(nseq):
            mine = (rows // t) == i
            s0 = s0_ref[i, h]
            o = o + _dot(jnp.where(mine, qe, 0.0).astype(BF16), s0.astype(BF16))
            e_col = _col_of_row(eb[i * t + t - 1:i * t + t, :], rows, cols)
            upd = _dot(jnp.where((cols // t) == i, kpt, 0.0).astype(BF16), vb)
            s_out_ref[i, h] = e_col * s0 + upd
        o_ref[:, h * HG_DV:(h + 1) * HG_DV] = _hg_readout(o, g, gain_ref[...])


def _hgrn_sample(hg_raw, lb, gain, state, layer, row0, batch, t):
    assert t == HG_DIAG and HG_ROWS % t == 0
    nseq = HG_ROWS // t
    blk0 = row0 // HG_ROWS
    return pl.pallas_call(
        functools.partial(_hgrn_sample_body, t=t),
        grid=(batch // nseq,),
        in_specs=[
            pl.BlockSpec((HG_ROWS, 4 * HG_WIDTH), lambda i: (blk0 + i, 0)),
            pl.BlockSpec((None, 1, HG_WIDTH), lambda i: (layer, 0, 0)),
            pl.BlockSpec((None, 1, HG_DV), lambda i: (layer, 0, 0)),
            pl.BlockSpec((None, nseq, HG_HEADS, HG_DK, HG_DV), lambda i: (layer, i, 0, 0, 0)),
        ],
        out_specs=[
            pl.BlockSpec((HG_ROWS, HG_WIDTH), lambda i: (i, 0)),
            pl.BlockSpec((nseq, HG_HEADS, HG_DK, HG_DV), lambda i: (i, 0, 0, 0)),
        ],
        out_shape=[jax.ShapeDtypeStruct((batch * t, HG_WIDTH), F32),
                   jax.ShapeDtypeStruct((batch, HG_HEADS, HG_DK, HG_DV), F32)],
        compiler_params=_params("parallel"),
        name="hgrn_sample",
    )(hg_raw, lb, gain, state)


def _stack_heads(x, heads, width):
    return jnp.concatenate([x[:, h * width:(h + 1) * width] for h in range(heads)], axis=0)


def _mla_up(o_lat, wuv_ref, o_ref, t):
    for h in range(MLA_HEADS):
        ol = o_lat[h * t:(h + 1) * t, :].astype(BF16)
        o_ref[:, h * MLA_VD:(h + 1) * MLA_VD] = _dot(ol, wuv_ref[h])


def _mla_prompt_body(ql_in_ref, qr_in_ref, c_ref, kr_ref, wuv_ref, o_ref,
                     ql_ref, qr_ref, m_ref, l_ref, acc_ref, *, tq, tk, scale):
    i, j = pl.program_id(1), pl.program_id(2)
    q0, k0 = i * tq, j * tk
    last_j = (q0 + tq - 1) // tk

    @pl.when(j == 0)
    def _():
        ql_ref[...] = _stack_heads(ql_in_ref[...], MLA_HEADS, KV_LORA).astype(BF16)
        qr_ref[...] = (_stack_heads(qr_in_ref[...], MLA_HEADS, MLA_ROPE) * scale).astype(BF16)
        m_ref[...] = jnp.full_like(m_ref, NEG_INF)
        l_ref[...] = jnp.zeros_like(l_ref)
        acc_ref[...] = jnp.zeros_like(acc_ref)

    def step(causal):
        cb = c_ref[...].astype(BF16)
        kb = kr_ref[...].astype(BF16)
        hr = MLA_HEADS * tq // MLA_ROW_GROUPS
        for part in range(MLA_ROW_GROUPS):
            r = slice(part * hr, (part + 1) * hr)
            s = _dot_nt(ql_ref[r, :], cb) + _dot_nt(qr_ref[r, :], kb)
            if causal:
                qpos = q0 + (_iota(s.shape, 0) & (tq - 1))
                s = jnp.where(k0 + _iota(s.shape, 1) <= qpos, s, NEG_INF)
            m_prev = m_ref[r, :]
            m_new = jnp.maximum(m_prev, jnp.max(s, axis=-1, keepdims=True))
            alpha = jnp.exp2(m_prev - m_new)
            p = jnp.exp2(s - m_new)
            l_ref[r, :] = alpha * l_ref[r, :] + jnp.sum(p, axis=-1, keepdims=True)
            acc_ref[r, :] = alpha * acc_ref[r, :] + _dot(p.astype(BF16), cb)
            m_ref[r, :] = m_new

    @pl.when(j < last_j)
    def _():
        step(False)

    @pl.when(j == last_j)
    def _():
        step(True)
        _mla_up(acc_ref[...] / l_ref[...], wuv_ref, o_ref, tq)


def _mla_prompt(q_lat, mqr, c, kr, wuv, layer, batch, seq, *, tq=128, tk=1024):
    tk = min(tk, seq)
    assert tk % tq == 0
    nq, nk = seq // tq, seq // tk
    scale = MLA_QSCALE
    hq = MLA_HEADS * tq

    def kv_map(b, i, j):
        return (b * nk + jnp.minimum(j, (i * tq + tq - 1) // tk), 0)

    return pl.pallas_call(
        functools.partial(_mla_prompt_body, tq=tq, tk=tk, scale=scale),
        grid=(batch, nq, nk),
        in_specs=[
            pl.BlockSpec((tq, MLA_HEADS * KV_LORA), lambda b, i, j: (b * nq + i, 0)),
            pl.BlockSpec((tq, MLA_HEADS * MLA_ROPE), lambda b, i, j: (b * nq + i, 0)),
            pl.BlockSpec((tk, KV_LORA), kv_map),
            pl.BlockSpec((tk, MLA_ROPE), kv_map),
            pl.BlockSpec((None, MLA_HEADS, KV_LORA, MLA_VD), lambda b, i, j: (layer, 0, 0, 0)),
        ],
        out_specs=pl.BlockSpec((tq, MLA_WIDTH), lambda b, i, j: (b * nq + i, 0)),
        out_shape=jax.ShapeDtypeStruct((batch * seq, MLA_WIDTH), F32),
        scratch_shapes=[pltpu.VMEM((hq, KV_LORA), BF16), pltpu.VMEM((hq, MLA_ROPE), BF16),
                        pltpu.VMEM((hq, 1), F32), pltpu.VMEM((hq, 1), F32),
                        pltpu.VMEM((hq, KV_LORA), F32)],
        compiler_params=_params("parallel", "parallel", "arbitrary"),
        name="mla_prompt",
    )(q_lat, mqr, c, kr, wuv)


NEW_PAD = 16


def _pad_rows(x, rows):
    return jnp.concatenate([x, jnp.zeros((rows - x.shape[0], x.shape[1]), x.dtype)], axis=0)


def _page_copies(pt_ref, pool_ref, buf_ref, sem_ref, layer, seq, slot, rows_last):
    def copy(p):
        rows = pl.ds(p * PAGE_SIZE, PAGE_SIZE)
        dst = buf_ref.at[(slot,) + (slice(None),) * (len(buf_ref.shape) - 2) + (rows,)] if rows_last \
            else buf_ref.at[slot, rows]
        return pltpu.make_async_copy(pool_ref.at[layer, pt_ref[seq, p]], dst, sem_ref.at[slot])
    return copy


def _fetch_pages(pt_ref, pools, sems, layer, n_pages, seq, slot, start):
    for k, (pool, buf, rows_last) in enumerate(pools):
        copy = _page_copies(pt_ref, pool, buf, sems.at[k], layer, seq, slot, rows_last)
        (_start_pages if start else _wait_pages)(copy, n_pages)


def _double_buffered_pages(pt_ref, pools, sems, layer, n_pages):
    b = pl.program_id(0)
    slot = b % 2

    @pl.when(b == 0)
    def _():
        _fetch_pages(pt_ref, pools, sems, layer, n_pages, 0, 0, True)

    @pl.when(b + 1 < pl.num_programs(0))
    def _():
        _fetch_pages(pt_ref, pools, sems, layer, n_pages, b + 1, 1 - slot, True)

    _fetch_pages(pt_ref, pools, sems, layer, n_pages, b, slot, False)
    return slot


def _softmax_part(s, pv, ok=None):
    if ok is not None:
        s = jnp.where(ok, s, NEG_INF)
    m = jnp.max(s, axis=-1, keepdims=True)
    p = jnp.exp2(s - m)
    if ok is not None:
        p = jnp.where(ok, p, 0.0)
    return m, jnp.sum(p, axis=-1, keepdims=True), pv(p.astype(BF16))


def _merge_parts(parts):
    m = parts[0][0]
    for part in parts[1:]:
        m = jnp.maximum(m, part[0])
    l, acc = 0.0, 0.0
    for mi, li, ai in parts:
        w = jnp.exp2(mi - m)
        l = l + w * li
        acc = acc + w * ai
    return acc / l


def _start_pages(copy, n_pages):
    def body(p, carry):
        copy(p).start()
        return carry
    lax.fori_loop(0, n_pages, body, 0)


def _wait_pages(copy, n_pages):
    def body(p, carry):
        copy(p).wait()
        return carry
    lax.fori_loop(0, n_pages, body, 0)


def _mla_sample_body(pt_ref, ql_in_ref, qr_in_ref, c_new_ref, kr_new_ref, wuv_ref, lat_hbm, krt_hbm,
                     o_ref, lat_buf, krt_buf, sems, *, layer, t, n_pages, chunk):
    pools = ((lat_hbm, lat_buf, False), (krt_hbm, krt_buf, True))
    slot = _double_buffered_pages(pt_ref, pools, sems, layer, n_pages)
    past = n_pages * PAGE_SIZE

    ql = _stack_heads(ql_in_ref[...], MLA_HEADS, KV_LORA).astype(BF16)
    qr = (_stack_heads(qr_in_ref[...], MLA_HEADS, MLA_ROPE) * MLA_QSCALE).astype(BF16)
    c_new = _pad_rows(c_new_ref[...], NEW_PAD).astype(BF16)
    kr_new = _pad_rows(kr_new_ref[...], NEW_PAD).astype(BF16)
    s_new = _dot_nt(ql, c_new) + _dot_nt(qr, kr_new)
    tok = _iota(s_new.shape, 0) & (t - 1)
    parts = [_softmax_part(s_new, lambda p: _dot(p, c_new), _iota(s_new.shape, 1) <= tok)]
    for ch in range(past // chunk):
        cb = lat_buf[slot, ch * chunk:(ch + 1) * chunk, :].astype(BF16)
        kb = krt_buf[slot, :, ch * chunk:(ch + 1) * chunk].astype(BF16)
        parts.append(_softmax_part(_dot_nt(ql, cb) + _dot(qr, kb), lambda p, cb=cb: _dot(p, cb)))
    _mla_up(_merge_parts(parts), wuv_ref, o_ref, t)


def _mla_sample(q_lat, mqr, c, kr, wuv, lat_pool, krt_pool, page_table, layer, row0, t, *, chunk=4096):
    batch, n_pages = page_table.shape
    assert t & (t - 1) == 0 and t <= NEW_PAD
    past = n_pages * PAGE_SIZE
    chunk = min(chunk, past)
    blk0 = row0 // t
    tok = lambda i, pt: (blk0 + i, 0)
    grid_spec = pltpu.PrefetchScalarGridSpec(
        num_scalar_prefetch=1,
        grid=(batch,),
        in_specs=[
            pl.BlockSpec((t, MLA_HEADS * KV_LORA), tok),
            pl.BlockSpec((t, MLA_HEADS * MLA_ROPE), tok),
            pl.BlockSpec((t, KV_LORA), tok),
            pl.BlockSpec((t, MLA_ROPE), tok),
            pl.BlockSpec((None, MLA_HEADS, KV_LORA, MLA_VD), lambda i, pt: (layer, 0, 0, 0)),
            pl.BlockSpec(memory_space=pl.ANY),
            pl.BlockSpec(memory_space=pl.ANY),
        ],
        out_specs=pl.BlockSpec((t, MLA_WIDTH), lambda i, pt: (i, 0)),
        scratch_shapes=[pltpu.VMEM((2, past, KV_LORA), F32), pltpu.VMEM((2, MLA_ROPE, past), F32),
                        pltpu.SemaphoreType.DMA((2, 2))],
    )
    return pl.pallas_call(
        functools.partial(_mla_sample_body, layer=layer, t=t, n_pages=n_pages, chunk=chunk),
        grid_spec=grid_spec,
        out_shape=jax.ShapeDtypeStruct((batch * t, MLA_WIDTH), F32),
        compiler_params=_params("arbitrary"),
        name="mla_sample",
    )(page_table, q_lat, mqr, c, kr, wuv, lat_pool, krt_pool)


def _block_means(x):
    pairs = x.reshape(x.shape[0] // SEL_BLOCK, SEL_BLOCK, x.shape[1])
    even = jnp.sum(pairs[:, :CMP_BLOCK, :], axis=1)
    odd = jnp.sum(pairs[:, CMP_BLOCK:, :], axis=1)
    return jnp.concatenate([even, odd], axis=0) / CMP_BLOCK


def _means_body(x_ref, o_ref):
    o_ref[...] = _block_means(x_ref[...])


def _nsa_means(kv_cmp, batch, seq):
    return pl.pallas_call(
        _means_body,
        grid=(batch,),
        in_specs=[pl.BlockSpec((seq, LANES), lambda b: (b, 0))],
        out_specs=pl.BlockSpec((None, seq // CMP_BLOCK, LANES), lambda b: (b, 0, 0)),
        out_shape=jax.ShapeDtypeStruct((batch, seq // CMP_BLOCK, LANES), F32),
        compiler_params=_params("parallel"),
        name="nsa_means",
    )(kv_cmp)


def _softmax_rows(s, mask):
    if mask is not None:
        s = jnp.where(mask, s, NEG_INF)
    m = jnp.max(s, axis=-1, keepdims=True)
    e = jnp.exp2(s - m)
    if mask is not None:
        e = jnp.where(mask, e, 0.0)
    l = jnp.sum(e, axis=-1, keepdims=True)
    return e / jnp.where(l > 0.0, l, 1.0)


def _select_blocks(imp, qpos, ns):
    blk = _iota(imp.shape, 1)
    cur = qpos // SEL_BLOCK
    forced = (blk == 0) | (blk == cur) | (blk == cur - 1)
    key = jnp.where(forced, imp + FORCED_IMPORTANCE, imp)
    key = jnp.where(blk * SEL_BLOCK <= qpos, key, NEG_INF)
    key = jnp.where(blk < ns, key, ABSENT)
    cnt = jnp.zeros(imp.shape, F32)
    for i in range(ns):
        col = key[:, i:i + 1]
        beats = (col > key) | ((col == key) & (blk > i))
        cnt = cnt + jnp.where(beats, 1.0, 0.0)
    return cnt < float(min(N_SEL, ns))


def _pad_lanes(x, lanes):
    if x.shape[1] == lanes:
        return x
    return jnp.concatenate([x, jnp.zeros((x.shape[0], lanes - x.shape[1]), x.dtype)], axis=1)


def _nsa_combine(gate, o_c, o_s, o_w, o_ref, t):
    for h in range(NSA_HEADS):
        r = slice(h * t, (h + 1) * t)
        o_ref[:, h * NSA_HD:(h + 1) * NSA_HD] = (gate[:, 3 * h:3 * h + 1] * o_c[r]
                                                 + gate[:, 3 * h + 1:3 * h + 2] * o_s[r]
                                                 + gate[:, 3 * h + 2:3 * h + 3] * o_w[r])


def _nsa_prompt_body(nq_ref, gate_ref, mean_ref, sel_ref, win_ref, exp_ref, o_ref, mp_ref, lp_ref, ap_ref,
                     *, tq, tk, seq, wlen):
    i = pl.program_id(1)
    q0 = i * tq
    rows = NSA_HEADS * tq
    q = (_stack_heads(nq_ref[...], NSA_HEADS, NSA_HD) * NSA_QSCALE).astype(BF16)
    tok_pos = q0 + (_iota((rows, 1), 0) & (tq - 1))
    qpos = q0 + _iota((tq, 1), 0)

    nc = seq // CMP_BLOCK
    ns = seq // SEL_BLOCK
    means = mean_ref[...]
    kc, vc = means[:, :NSA_HD].astype(BF16), means[:, NSA_HD:].astype(BF16)
    s_c = _dot_nt(q, kc)
    col = _iota(s_c.shape, 1)
    cid = jnp.where(col < ns, 2 * col, 2 * (col - ns) + 1)
    p_c = _softmax_rows(s_c, (cid + 1) * CMP_BLOCK - 1 <= tok_pos)
    o_c = _dot(p_c.astype(BF16), vc)
    imp = jnp.sum(p_c.reshape(NSA_HEADS, tq, nc), axis=0)
    lanes = -(-ns // LANES) * LANES
    imp = _pad_lanes(imp[:, :ns] + imp[:, ns:], lanes)
    chosen = _select_blocks(imp, qpos, ns) & (_iota((tq, lanes), 1) * SEL_BLOCK <= qpos)
    bias = jnp.where(chosen, 0.0, NEG_INF)

    nb = tk // SEL_BLOCK
    last_j = q0 // tk
    for j in range(seq // tk):
        def part(causal, j=j):
            kv = sel_ref[j * tk:(j + 1) * tk, :]
            ks, vs = kv[:, :NSA_HD].astype(BF16), kv[:, NSA_HD:].astype(BF16)
            tile_bias = _dot(bias[:, j * nb:(j + 1) * nb].astype(BF16), exp_ref[...])
            if causal:
                tile_bias = jnp.where(j * tk + _iota((tq, tk), 1) <= qpos, tile_bias, NEG_INF)
            hh = NSA_HEADS // NSA_HEAD_GROUPS
            for grp in range(NSA_HEAD_GROUPS):
                r = slice(grp * hh * tq, (grp + 1) * hh * tq)
                s = (_dot_nt(q[r], ks).reshape(hh, tq, tk) + tile_bias[None]).reshape(hh * tq, tk)
                mp_ref[j, r, :], lp_ref[j, r, :], ap_ref[j, r, :] = _softmax_part(s, lambda p: _dot(p, vs))

        @pl.when(j < last_j)
        def _():
            part(False)

        @pl.when(j == last_j)
        def _():
            part(True)

        @pl.when(j > last_j)
        def _():
            mp_ref[j] = jnp.full((rows, 1), NEG_INF, F32)
            lp_ref[j] = jnp.zeros((rows, 1), F32)
            ap_ref[j] = jnp.zeros((rows, NSA_HD), F32)

    o_s = _merge_parts([(mp_ref[j], lp_ref[j], ap_ref[j]) for j in range(seq // tk)])

    w0 = pl.multiple_of(jnp.clip(q0 + tq - wlen, 0, seq - wlen), SUBLANES)
    wkv = win_ref[pl.ds(w0, wlen), :]
    kw, vw = wkv[:, :NSA_HD].astype(BF16), wkv[:, NSA_HD:].astype(BF16)
    wpos = w0 + _iota((tq, wlen), 1)
    bias_w = jnp.where((wpos <= qpos) & (wpos > qpos - WINDOW), 0.0, NEG_INF)
    s_w = (_dot_nt(q, kw).reshape(NSA_HEADS, tq, wlen) + bias_w[None]).reshape(rows, wlen)
    _, l_w, a_w = _softmax_part(s_w, lambda p: _dot(p, vw))
    o_w = a_w / l_w
    _nsa_combine(gate_ref[...], o_c, o_s, o_w, o_ref, tq)


def _expand_matrix(lanes, keys, tile):
    e = (np.arange(keys)[None, :] // SEL_BLOCK) == np.arange(lanes)[:, None]
    e = e.reshape(lanes, keys // tile, tile).transpose(1, 0, 2)
    return jnp.asarray(e, BF16)


def _nsa_prompt(nq, gates, means, kv_sel, kv_win, batch, seq, *, tq=Q_BLOCK, tk=1024):
    tk = min(tk, seq)
    assert tk % tq == 0 and seq % tk == 0
    nqb = seq // tq
    nk = seq // tk
    rows = NSA_HEADS * tq
    wlen = min(WINDOW + tq, seq)
    expand = _expand_matrix(tk // SEL_BLOCK, tk, tk)[0]
    tok = lambda b, i: (b * nqb + i, 0)
    return pl.pallas_call(
        functools.partial(_nsa_prompt_body, tq=tq, tk=tk, seq=seq, wlen=wlen),
        grid=(batch, nqb),
        in_specs=[
            pl.BlockSpec((tq, NSA_WIDTH), tok),
            pl.BlockSpec((tq, LANES), tok),
            pl.BlockSpec((None, seq // CMP_BLOCK, LANES), lambda b, i: (b, 0, 0)),
            pl.BlockSpec((seq, LANES), lambda b, i: (b, 0)),
            pl.BlockSpec((seq, LANES), lambda b, i: (b, 0)),
            pl.BlockSpec(expand.shape, lambda b, i: (0, 0)),
        ],
        out_specs=pl.BlockSpec((tq, NSA_WIDTH), tok),
        out_shape=jax.ShapeDtypeStruct((batch * seq, NSA_WIDTH), F32),
        scratch_shapes=[pltpu.VMEM((nk, rows, 1), F32), pltpu.VMEM((nk, rows, 1), F32),
                        pltpu.VMEM((nk, rows, NSA_HD), F32)],
        compiler_params=_params("parallel", "parallel"),
        name="nsa_prompt",
    )(nq, gates, means, kv_sel, kv_win, expand)


def _nsa_sample_body(pt_ref, nq_ref, gate_ref, sel_new_ref, win_new_ref, wbuf_ref, pool_ref, exp_ref,
                     cmp_hbm, sel_hbm, o_ref, cmp_buf, sel_buf, sems, *, layer, t, n_pages, chunk):
    pools = ((cmp_hbm, cmp_buf, True), (sel_hbm, sel_buf, True))
    slot = _double_buffered_pages(pt_ref, pools, sems, layer, n_pages)
    past = n_pages * PAGE_SIZE
    rows = NSA_HEADS * t
    q = (_stack_heads(nq_ref[...], NSA_HEADS, NSA_HD) * NSA_QSCALE).astype(BF16)
    tok = _iota((rows, 1), 0) & (t - 1)

    ncp = past // CMP_BLOCK
    nsp = past // SEL_BLOCK
    ns = nsp + 1
    cw = chunk // CMP_BLOCK
    means = []
    for ch in range(past // chunk):
        x = cmp_buf[slot, :, :, ch * chunk:(ch + 1) * chunk].reshape(2 * NSA_HD, chunk)
        hi = x.astype(BF16)
        lo = (x - hi.astype(F32)).astype(BF16)
        pooled = _dot(jnp.concatenate([hi, lo], axis=0), pool_ref[...])
        means.append(pooled[:2 * NSA_HD] + pooled[2 * NSA_HD:])
    means = jnp.concatenate(means, axis=1)
    kct, vct = means[:NSA_HD].astype(BF16), means[NSA_HD:].astype(BF16)
    s_c = _dot(q, kct)
    e_c = jnp.exp2(s_c - jnp.max(s_c, axis=-1, keepdims=True))
    p_c = e_c / jnp.sum(e_c, axis=-1, keepdims=True)
    o_c = _dot_nt(p_c.astype(BF16), vct)
    imp = jnp.sum(p_c.reshape(NSA_HEADS, t, ncp), axis=0)
    imp = jnp.concatenate([imp[:, c * cw:c * cw + cw // 2] + imp[:, c * cw + cw // 2:(c + 1) * cw]
                           for c in range(past // chunk)], axis=1)
    lanes = -(-ns // LANES) * LANES
    imp = _pad_lanes(imp, lanes)
    chosen = _select_blocks(imp, past + _iota((t, 1), 0), ns)
    bias = jnp.concatenate([jnp.where(chosen, 0.0, NEG_INF)] * NSA_HEADS, axis=0)

    sel_new = _pad_rows(sel_new_ref[...], NEW_PAD)
    k_new, v_new = sel_new[:, :NSA_HD].astype(BF16), sel_new[:, NSA_HD:].astype(BF16)
    s_new = _dot_nt(q, k_new)
    ok_new = (_iota(s_new.shape, 1) <= tok) & (bias[:, nsp:nsp + 1] > -1.0)
    parts = [_softmax_part(s_new, lambda p: _dot(p, v_new), ok_new)]
    nb = chunk // SEL_BLOCK
    for ch in range(past // chunk):
        kt = sel_buf[slot, 0, :, ch * chunk:(ch + 1) * chunk].astype(BF16)
        vt = sel_buf[slot, 1, :, ch * chunk:(ch + 1) * chunk].astype(BF16)
        s = _dot(q, kt) + _dot(bias[:, ch * nb:(ch + 1) * nb].astype(BF16), exp_ref[...])
        parts.append(_softmax_part(s, lambda p, vt=vt: _dot_nt(p, vt)))
    o_s = _merge_parts(parts)

    wb = wbuf_ref.shape[-1]
    win_new = _pad_rows(win_new_ref[...], NEW_PAD)
    kw_new, vw_new = win_new[:, :NSA_HD].astype(BF16), win_new[:, NSA_HD:].astype(BF16)
    s_n = _dot_nt(q, kw_new)
    vwt = wbuf_ref[1].astype(BF16)
    s_b = _dot(q, wbuf_ref[0].astype(BF16))
    o_w = _merge_parts([
        _softmax_part(s_n, lambda p: _dot(p, vw_new), _iota(s_n.shape, 1) <= tok),
        _softmax_part(s_b, lambda p: _dot_nt(p, vwt), _iota(s_b.shape, 1) > tok + (wb - WINDOW))])
    _nsa_combine(gate_ref[...], o_c, o_s, o_w, o_ref, t)


def _pool_matrix(chunk):
    half = chunk // SEL_BLOCK
    blk = np.arange(chunk) // CMP_BLOCK
    col = np.where(blk % 2 == 0, blk // 2, half + blk // 2)
    p = np.zeros((chunk, chunk // CMP_BLOCK), np.float32)
    p[np.arange(chunk), col] = 1.0 / CMP_BLOCK
    return jnp.asarray(p, BF16)


def _nsa_sample(nq, gates, kv_sel, kv_win, win_t, cmp_pool, sel_pool, page_table, layer, row0, t, *, chunk=4096):
    batch, n_pages = page_table.shape
    assert t & (t - 1) == 0 and t <= NEW_PAD and t < CMP_BLOCK
    past = n_pages * PAGE_SIZE
    chunk = min(chunk, past)
    blk0 = row0 // t
    wb = win_t.shape[-1]
    pool = _pool_matrix(chunk)
    expand = _expand_matrix(chunk // SEL_BLOCK, chunk, chunk)[0]
    tok = lambda i, pt: (blk0 + i, 0)
    grid_spec = pltpu.PrefetchScalarGridSpec(
        num_scalar_prefetch=1,
        grid=(batch,),
        in_specs=[
            pl.BlockSpec((t, NSA_WIDTH), tok),
            pl.BlockSpec((t, LANES), tok),
            pl.BlockSpec((t, LANES), tok),
            pl.BlockSpec((t, LANES), tok),
            pl.BlockSpec((None, None, 2, NSA_HD, wb), lambda i, pt: (layer, i, 0, 0, 0)),
            pl.BlockSpec(pool.shape, lambda i, pt: (0, 0)),
            pl.BlockSpec(expand.shape, lambda i, pt: (0, 0)),
            pl.BlockSpec(memory_space=pl.ANY),
            pl.BlockSpec(memory_space=pl.ANY),
        ],
        out_specs=pl.BlockSpec((t, NSA_WIDTH), lambda i, pt: (i, 0)),
        scratch_shapes=[pltpu.VMEM((2, 2, NSA_HD, past), F32), pltpu.VMEM((2, 2, NSA_HD, past), F32),
                        pltpu.SemaphoreType.DMA((2, 2))],
    )
    return pl.pallas_call(
        functools.partial(_nsa_sample_body, layer=layer, t=t, n_pages=n_pages, chunk=chunk),
        grid_spec=grid_spec,
        out_shape=jax.ShapeDtypeStruct((batch * t, NSA_WIDTH), F32),
        compiler_params=_params("arbitrary"),
        name="nsa_sample",
    )(page_table, nq, gates, kv_sel, kv_win, win_t, pool, expand, cmp_pool, sel_pool)


def _pack_w_in(w_in):
    d_model = w_in.shape[1]
    cuts = np.cumsum((HG_HEADS * HG_DK, HG_HEADS * HG_DK, HG_WIDTH, HG_WIDTH, NSA_WIDTH, 2 * NSA_HD,
                      2 * NSA_HD, 2 * NSA_HD, 3 * NSA_HEADS, MLA_HEADS * (MLA_NOPE + MLA_ROPE),
                      KV_LORA, MLA_ROPE, N_BRANCH * d_model)).tolist()
    w = w_in.astype(BF16)
    depth = w.shape[0]
    w_hg = w[:, :, :cuts[3]]
    nsa = w[:, :, cuts[3]:cuts[7]]
    ngate = w[:, :, cuts[7]:cuts[8]]
    mq = w[:, :, cuts[8]:cuts[9]].reshape(depth, d_model, MLA_HEADS, MLA_NOPE + MLA_ROPE)
    mqn = mq[..., :MLA_NOPE].reshape(depth, d_model, MLA_HEADS * MLA_NOPE)
    mqr = mq[..., MLA_NOPE:].reshape(depth, d_model, MLA_HEADS * MLA_ROPE)
    mc = w[:, :, cuts[9]:cuts[10]]
    mkr = w[:, :, cuts[10]:cuts[11]]
    pad = lambda k: jnp.zeros((depth, d_model, k), BF16)
    w_nm = jnp.concatenate([nsa, ngate, pad(LANES - 3 * NSA_HEADS), mqn, mqr, mc, mkr,
                            pad(LANES - MLA_ROPE)], axis=-1)
    w_gate = w[:, :, cuts[11]:]
    return w_hg, w_nm, w_gate


def _rope_tables(pos):
    n = pos.shape[0]
    posf = pos.astype(F32)[:, None]

    def cs(half):
        inv_freq = ROPE_THETA ** (-jnp.arange(half, dtype=F32) / half)
        ang = posf * inv_freq[None, :]
        return jnp.cos(ang), jnp.sin(ang)

    cn, sn = cs(NSA_ROT // 2)
    zn = jnp.zeros_like(sn)
    rest = NSA_HD - NSA_ROT
    head = (jnp.concatenate([cn, cn, jnp.ones((n, rest), F32)], 1),
            jnp.concatenate([-sn, zn, jnp.zeros((n, rest), F32)], 1),
            jnp.concatenate([zn, sn, jnp.zeros((n, rest), F32)], 1))
    ident = (jnp.ones((n, NSA_HD), F32), jnp.zeros((n, NSA_HD), F32), jnp.zeros((n, NSA_HD), F32))
    q_tabs = [jnp.tile(t, (1, LANES // NSA_HD)) for t in head]
    kv_tabs = [jnp.concatenate([t, i], 1) for t, i in zip(head, ident)]
    cm, sm = cs(MLA_ROPE // 2)
    zm = jnp.zeros_like(sm)
    grp = (jnp.concatenate([cm, cm], 1), jnp.concatenate([-sm, zm], 1), jnp.concatenate([zm, sm], 1))
    m_tabs = [jnp.tile(t, (1, LANES // MLA_ROPE)) for t in grp]
    return jnp.stack(q_tabs + kv_tabs + m_tabs)


def kernel(x_prompt, x_sample, cache_cmp_kv, cache_sel_kv, cache_mla_latent, cache_mla_krope, cache_win_kv,
           state_hgrn, page_table, norm_g, w_ffn_gu, w_ffn_dn, w_in, hg_lb, hg_norm, mla_norm, w_uk, w_uv,
           w_br, w_out):
    batch, seq, d = x_prompt.shape
    bs, ts, _ = x_sample.shape
    depth = norm_g.shape[0]
    n_pool = cache_cmp_kv.shape[1]
    n_pages = page_table.shape[1]
    past = n_pages * PAGE_SIZE
    wb = cache_win_kv.shape[2]
    rp = batch * seq
    n = rp + bs * ts
    assert n % 512 == 0 and rp % 512 == 0 and seq % 512 == 0 and bs % (HG_ROWS // ts) == 0

    x = jnp.concatenate([x_prompt.reshape(rp, d), x_sample.reshape(bs * ts, d)], axis=0)
    pos = jnp.concatenate([jnp.tile(jnp.arange(seq, dtype=jnp.int32), batch),
                           jnp.tile(past + jnp.arange(ts, dtype=jnp.int32), bs)])
    tabs = _rope_tables(pos)
    ng = norm_g.reshape(depth, 6, 1, d)
    w_gu = w_ffn_gu.astype(BF16)
    w_dn = w_ffn_dn.astype(BF16)
    w_hg, w_nm, w_gate = _pack_w_in(w_in)
    wuk_t = jnp.transpose(w_uk, (0, 2, 3, 1)).astype(BF16)
    wuv_t = jnp.transpose(w_uv, (0, 2, 1, 3)).astype(BF16)
    w_brb = w_br.astype(BF16)
    w_outb = w_out.astype(BF16)
    lb_all = _hgrn_lower_bounds(hg_lb)[:, None, :]
    hg_gain = hg_norm[:, None, :]
    mla_gain = mla_norm[:, None, :]
    cmp_pool = jnp.transpose(cache_cmp_kv, (0, 1, 3, 4, 2))
    sel_pool = jnp.transpose(cache_sel_kv, (0, 1, 3, 4, 2))
    krt_pool = jnp.transpose(cache_mla_krope, (0, 1, 3, 2))
    win_t = jnp.transpose(cache_win_kv, (0, 1, 3, 4, 2))

    st_p, st_s = [], []
    for l in range(depth):
        x = _ffn(x, ng, w_gu, w_dn, l, 0)
        hg_raw = _proj_hg(x, ng, w_hg, l)
        nq, gates, kv_cmp, kv_sel, kv_win, mqn, mqr, c, kr = _proj_nm(x, ng, w_nm, tabs, mla_gain, l)
        q_lat = _mla_absorb(mqn, wuk_t, l)

        o_hg_p, s_p = _hgrn_prompt(hg_raw, lb_all, hg_gain, l, batch, seq)
        o_hg_s, s_s = _hgrn_sample(hg_raw, lb_all, hg_gain, state_hgrn, l, rp, bs, ts)
        means = _nsa_means(kv_cmp, batch, seq)
        o_nsa_p = _nsa_prompt(nq, gates, means, kv_sel, kv_win, batch, seq)
        o_nsa_s = _nsa_sample(nq, gates, kv_sel, kv_win, win_t, cmp_pool, sel_pool, page_table, l, rp, ts)
        o_mla_p = _mla_prompt(q_lat, mqr, c, kr, wuv_t, l, batch, seq)
        o_mla_s = _mla_sample(q_lat, mqr, c, kr, wuv_t, cache_mla_latent, krt_pool, page_table, l, rp, ts)

        o_hg = jnp.concatenate([o_hg_p, o_hg_s], axis=0)
        o_nsa = jnp.concatenate([o_nsa_p, o_nsa_s], axis=0)
        o_mla = jnp.concatenate([o_mla_p, o_mla_s], axis=0)
        y = _merge(x, ng, o_hg, o_nsa, o_mla, w_gate, w_brb, l)
        x = _out_proj(x, y, ng, w_outb, l)
        x = _ffn(x, ng, w_gu, w_dn, l, 1)

        kv4 = lambda a, b, t: a.reshape(b, t, 2, NSA_HD)
        wp = min(WINDOW, seq)
        st_p.append((kv4(kv_cmp[:rp], batch, seq), kv4(kv_sel[:rp], batch, seq),
                     kv4(kv_win[:rp], batch, seq)[:, seq - wp:], c[:rp].reshape(batch, seq, KV_LORA),
                     kr[:rp].reshape(batch, seq, MLA_ROPE), s_p))
        win_new_t = jnp.transpose(kv4(kv_win[rp:], bs, ts), (0, 2, 3, 1))
        w_all_t = jnp.concatenate([win_t[l][..., ts:], win_new_t], axis=-1)
        st_s.append((kv4(kv_cmp[rp:], bs, ts), kv4(kv_sel[rp:], bs, ts), jnp.transpose(w_all_t, (0, 3, 1, 2)),
                     c[rp:].reshape(bs, ts, KV_LORA), kr[rp:].reshape(bs, ts, MLA_ROPE), s_s))

    stack = lambda states, i: jnp.stack([s[i] for s in states])
    return ((x[:rp].reshape(batch, seq, d), x[rp:].reshape(bs, ts, d))
            + tuple(stack(st_p, i) for i in range(6)) + tuple(stack(st_s, i) for i in range(6)))
```
